```python
import math
import jax
import jax.numpy as jnp
from jax import lax
import numpy as np

D_MODEL = 2048
BATCH = 2
SEQ = 4096
DEPTH = 2
DEC_BATCH = 128
DEC_SEQ = 4
PAST_LEN = 8192
PAGE_SIZE = 128

MA_HEADS = 8
MA_V = D_MODEL // MA_HEADS
MA_QK = MA_V // 2
MA_WIDTH = MA_HEADS * MA_V
MLSTM_CHUNK = 64
F_BIAS_LO = 3.0
F_BIAS_HI = 6.0
MLA_NOPE = 128
MLA_ROPE = 64
MLA_VDIM = 128
MLA_HEADS = D_MODEL // MLA_VDIM
MLA_WIDTH = MLA_HEADS * MLA_VDIM
Q_RANK = 512
KV_RANK = 512
ROPE_THETA = 10000.0
MLA_SCALE = (MLA_NOPE + MLA_ROPE) ** -0.5
Q_BLOCK = 128
PLE_DIM = 256
EPS = 1e-6

SPLIT_SIZES = (MA_HEADS * MA_QK,
               MA_HEADS * MA_QK,
               MA_WIDTH,
               MA_HEADS,
               MA_HEADS,
               MA_WIDTH,
               MA_WIDTH,
               Q_RANK,
               KV_RANK,
               MLA_ROPE,
               MLA_WIDTH,
               D_MODEL,
               D_MODEL)
IN_WIDTH = sum(SPLIT_SIZES)
SPLIT_IDX = tuple(int(v) for v in np.cumsum(SPLIT_SIZES)[:-1])

kernel_name = "hybrid_mlstm_mla_step"


def _rmsnorm(x, g):
    xf = x.astype(jnp.float32)
    y = xf * lax.rsqrt(jnp.mean(xf * xf, axis=-1, keepdims=True) + EPS)
    return (y * g.astype(jnp.float32)).astype(x.dtype)


def _headnorm(h, g):
    B, S, _ = h.shape
    hf = h.astype(jnp.float32).reshape(B, S, MA_HEADS, MA_V)
    hf = hf * lax.rsqrt(jnp.mean(hf * hf, axis=-1, keepdims=True) + EPS)
    return (hf.reshape(B, S, MA_WIDTH) * g.astype(jnp.float32)).astype(h.dtype)


def _rope(x, pos):
    half = MLA_ROPE // 2
    inv = ROPE_THETA ** (-jnp.arange(half, dtype=jnp.float32) / half)
    ang = pos[:, None] * inv[None, :]
    cos = jnp.cos(ang)[None, :, None, :]
    sin = jnp.sin(ang)[None, :, None, :]
    xf = x.astype(jnp.float32)
    x1, x2 = xf[..., :half], xf[..., half:]
    return jnp.concatenate([x1 * cos - x2 * sin, x2 * cos + x1 * sin], axis=-1).astype(x.dtype)


def _mlstm(q, k, v, i_pre, f_pre, C0, n0, m0):
    B, S, H, dk = q.shape
    L = MLSTM_CHUNK if S % MLSTM_CHUNK == 0 else S
    nc = S // L
    f32 = jnp.float32

    def to_chunks(a):
        a = a.astype(f32).reshape((B, nc, L) + a.shape[2:])
        return jnp.swapaxes(jnp.moveaxis(a, 1, 0), 2, 3)

    qc = to_chunks(q.astype(f32) * (dk ** -0.5))
    kc = to_chunks(k)
    vc = to_chunks(v)
    ic = to_chunks(i_pre)
    fc = to_chunks(jax.nn.log_sigmoid(f_pre.astype(f32)))
    causal = jnp.tril(jnp.ones((L, L), dtype=bool))

    def step(carry, xs):
        C, n, m = carry
        qt, kt, vt, it, ft = xs
        b = jnp.cumsum(ft, axis=-1)
        d = jnp.where(causal, b[..., :, None] - b[..., None, :] + it[..., None, :], -jnp.inf)
        inter = b + m[..., None]
        m_t = jnp.maximum(inter, jnp.max(d, axis=-1))
        s_inter = jnp.exp(inter - m_t)
        a = jnp.einsum('bhtd,bhsd->bhts', qt, kt) * jnp.exp(d - m_t[..., None])
        num = s_inter[..., None] * jnp.einsum('bhvd,bhtd->bhtv', C, qt) + jnp.einsum('bhts,bhsv->bhtv', a, vt)
        den = s_inter * jnp.einsum('bhd,bhtd->bht', n, qt) + jnp.sum(a, axis=-1)
        h = num / jnp.maximum(jnp.abs(den), jnp.exp(-m_t))[..., None]
        m_new = m_t[..., -1]
        g_state = jnp.exp(b[..., -1] + m - m_new)
        g_tok = jnp.exp(b[..., -1:] - b + it - m_new[..., None])
        C = g_state[..., None, None] * C + jnp.einsum('bhs,bhsv,bhsd->bhvd', g_tok, vt, kt)
        n = g_state[..., None] * n + jnp.einsum('bhs,bhsd->bhd', g_tok, kt)
        return (C, n, m_new), h

    (C, n, m), hs = lax.scan(step, (C0.astype(f32), n0.astype(f32), m0.astype(f32)), (qc, kc, vc, ic, fc))
    h = jnp.moveaxis(jnp.swapaxes(hs, 2, 3), 0, 1).reshape(B, S, H, -1)
    return h.astype(q.dtype), C, n, m


def _scores(q_lat, q_pe, ckv, kpe):
    s = (jnp.einsum('bqhc,bkc->bhqk', q_lat, ckv, preferred_element_type=jnp.float32)
         + jnp.einsum('bqhr,bkr->bhqk', q_pe, kpe, preferred_element_type=jnp.float32))
    return s * MLA_SCALE


def _prompt_attn(q_lat, q_pe, ckv, kpe):
    B, S, H, C = q_lat.shape
    nb = S // Q_BLOCK

    def blocks(a):
        return jnp.moveaxis(a.reshape((B, nb, Q_BLOCK) + a.shape[2:]), 1, 0)

    k_pos = jnp.arange(S)

    def one_block(args):
        ql, qp, blk = args
        q_pos = blk * Q_BLOCK + jnp.arange(Q_BLOCK)
        s = jnp.where(k_pos[None, :] <= q_pos[:, None], _scores(ql, qp, ckv, kpe), -jnp.inf)
        w = jax.nn.softmax(s, axis=-1)
        return jnp.einsum('bhqk,bkc->bqhc', w.astype(ckv.dtype), ckv, preferred_element_type=jnp.float32)

    o = lax.map(one_block, (blocks(q_lat), blocks(q_pe), jnp.arange(nb)))
    return jnp.moveaxis(o, 0, 1).reshape(B, S, H, C).astype(q_lat.dtype)


def _sample_attn(q_lat, q_pe, ckv_past, kpe_past, ckv_new, kpe_new):
    T = q_lat.shape[1]
    P = ckv_past.shape[1]
    s_past = _scores(q_lat, q_pe, ckv_past, kpe_past)
    s_new = jnp.where(jnp.tril(jnp.ones((T, T), dtype=bool)),
                      _scores(q_lat, q_pe, ckv_new, kpe_new), -jnp.inf)
    w = jax.nn.softmax(jnp.concatenate([s_past, s_new], axis=-1), axis=-1)
    o = (jnp.einsum('bhtp,bpc->bthc', w[..., :P].astype(ckv_past.dtype), ckv_past, preferred_element_type=jnp.float32)
         + jnp.einsum('bhts,bsc->bthc', w[..., P:].astype(ckv_new.dtype), ckv_new, preferred_element_type=jnp.float32))
    return o.astype(q_lat.dtype)


def _layer(x, p, pos, C0, n0, m0, past, g_in, w_in, b_i, b_f, g_q, w_uq, g_kv, w_uk, w_uv,
           g_head, w_pa, w_pb, w_out, w_ple, g_ple, w_pg):
    B, S, _ = x.shape
    h = _rmsnorm(x, g_in)
    u = h @ w_in
    (q_a, k_a, v_a, i_a, f_a, o_a, z_a, cq, ckv, kr, z_b, g_a, g_b) = jnp.split(u, SPLIT_IDX, axis=-1)

    h_a, C, n, m = _mlstm(q_a.reshape(B, S, MA_HEADS, MA_QK), k_a.reshape(B, S, MA_HEADS, MA_QK),
                          v_a.reshape(B, S, MA_HEADS, MA_V), i_a + b_i, f_a + b_f, C0, n0, m0)
    h_a = _headnorm(h_a.reshape(B, S, MA_WIDTH) * jax.nn.sigmoid(o_a), g_head)
    y_a = (h_a * jax.nn.silu(z_a)) @ w_pa

    c_q = _rmsnorm(cq, g_q)
    qf = jnp.einsum('bsc,chd->bshd', c_q, w_uq)
    q_nope, q_pe = qf[..., :MLA_NOPE], _rope(qf[..., MLA_NOPE:], pos)
    c_kv = _rmsnorm(ckv, g_kv)
    k_pe = _rope(kr[:, :, None, :], pos)[:, :, 0, :]
    q_lat = jnp.einsum('bshd,chd->bshc', q_nope, w_uk)
    if past is None:
        o_lat = _prompt_attn(q_lat, q_pe, c_kv, k_pe)
    else:
        o_lat = _sample_attn(q_lat, q_pe, past[0], past[1], c_kv, k_pe)
    o_b = jnp.einsum('bshc,chd->bshd', o_lat, w_uv).reshape(B, S, MLA_WIDTH)
    y_b = (o_b * jax.nn.silu(z_b)) @ w_pb

    mix = jax.nn.sigmoid(g_a) * y_a + jax.nn.sigmoid(g_b) * y_b
    x = x + mix @ w_out
    x = x + (p @ w_ple) * jax.nn.sigmoid(_rmsnorm(x, g_ple) @ w_pg)
    return x, c_kv, k_pe, C, n, m


def setup_inputs(seed: int = 0) -> dict:
    key = jax.random.key(seed)
    ks = jax.random.split(key, 32)
    f32 = jnp.float32
    n_pages = PAST_LEN // PAGE_SIZE
    n_pool = (DEC_BATCH * n_pages * 5) // 4

    def nrm(k, shape, scale=1.0):
        return jax.random.normal(k, shape, f32) * scale

    def gain(k, shape):
        return 1.0 + 0.02 * jax.random.normal(k, shape, f32)

    page_table = jax.random.permutation(ks[9], n_pool)[:DEC_BATCH * n_pages].reshape(DEC_BATCH, n_pages).astype(jnp.int32)
    b_f = jnp.linspace(F_BIAS_LO, F_BIAS_HI, MA_HEADS, dtype=f32)[None, :] + 0.1 * jax.random.normal(ks[13], (DEPTH, MA_HEADS), f32)
    return {
        'x_prompt': nrm(ks[0], (BATCH, SEQ, D_MODEL)),
        'x_sample': nrm(ks[1], (DEC_BATCH, DEC_SEQ, D_MODEL)),
        'p_prompt': nrm(ks[2], (DEPTH, BATCH, SEQ, PLE_DIM)),
        'p_sample': nrm(ks[3], (DEPTH, DEC_BATCH, DEC_SEQ, PLE_DIM)),
        'cache_ckv': nrm(ks[4], (DEPTH, n_pool, PAGE_SIZE, KV_RANK)),
        'cache_kpe': nrm(ks[5], (DEPTH, n_pool, PAGE_SIZE, MLA_ROPE)),
        'state_C': nrm(ks[6], (DEPTH, DEC_BATCH, MA_HEADS, MA_V, MA_QK)),
        'state_n': nrm(ks[7], (DEPTH, DEC_BATCH, MA_HEADS, MA_QK)),
        'state_m': nrm(ks[8], (DEPTH, DEC_BATCH, MA_HEADS)),
        'page_table': page_table,
        'g_in': gain(ks[10], (DEPTH, D_MODEL)),
        'w_in': nrm(ks[11], (DEPTH, D_MODEL, IN_WIDTH), D_MODEL ** -0.5),
        'b_i': nrm(ks[12], (DEPTH, MA_HEADS), 0.1),
        'b_f': b_f,
        'g_q': gain(ks[14], (DEPTH, Q_RANK)),
        'w_uq': nrm(ks[15], (DEPTH, Q_RANK, MLA_HEADS, MLA_NOPE + MLA_ROPE), Q_RANK ** -0.5),
        'g_kv': gain(ks[16], (DEPTH, KV_RANK)),
        'w_uk': nrm(ks[17], (DEPTH, KV_RANK, MLA_HEADS, MLA_NOPE), KV_RANK ** -0.5),
        'w_uv': nrm(ks[18], (DEPTH, KV_RANK, MLA_HEADS, MLA_VDIM), KV_RANK ** -0.5),
        'g_head': gain(ks[19], (DEPTH, MA_WIDTH)),
        'w_pa': nrm(ks[20], (DEPTH, MA_WIDTH, D_MODEL), MA_WIDTH ** -0.5),
        'w_pb': nrm(ks[21], (DEPTH, MLA_WIDTH, D_MODEL), MLA_WIDTH ** -0.5),
        'w_out': nrm(ks[22], (DEPTH, D_MODEL, D_MODEL), D_MODEL ** -0.5),
        'w_ple': nrm(ks[23], (DEPTH, PLE_DIM, D_MODEL), PLE_DIM ** -0.5),
        'g_ple': gain(ks[24], (DEPTH, D_MODEL)),
        'w_pg': nrm(ks[25], (DEPTH, D_MODEL, D_MODEL), D_MODEL ** -0.5),
        'g_final': gain(ks[26], (D_MODEL,)),
    }


def reference(x_prompt, x_sample, p_prompt, p_sample, cache_ckv, cache_kpe, state_C, state_n, state_m,
              page_table, g_in, w_in, b_i, b_f, g_q, w_uq, g_kv, w_uk, w_uv, g_head, w_pa, w_pb,
              w_out, w_ple, g_ple, w_pg, g_final):
    B, S, _ = x_prompt.shape
    Bd, T, _ = x_sample.shape
    past_len = page_table.shape[1] * PAGE_SIZE
    pos_p = jnp.arange(S, dtype=jnp.float32)
    pos_s = past_len + jnp.arange(T, dtype=jnp.float32)
    C0 = jnp.zeros((B, MA_HEADS, MA_V, MA_QK), jnp.float32)
    n0 = jnp.zeros((B, MA_HEADS, MA_QK), jnp.float32)
    m0 = jnp.zeros((B, MA_HEADS), jnp.float32)

    xp, xs = x_prompt, x_sample
    ckv_p, kpe_p, Cp, np_, mp = [], [], [], [], []
    ckv_s, kpe_s, Cs, ns, ms = [], [], [], [], []
    for l in range(DEPTH):
        lw = (g_in[l], w_in[l], b_i[l], b_f[l], g_q[l], w_uq[l], g_kv[l], w_uk[l], w_uv[l],
              g_head[l], w_pa[l], w_pb[l], w_out[l], w_ple[l], g_ple[l], w_pg[l])
        xp, a1, a2, a3, a4, a5 = _layer(xp, p_prompt[l], pos_p, C0, n0, m0, None, *lw)
        ckv_p.append(a1); kpe_p.append(a2); Cp.append(a3); np_.append(a4); mp.append(a5)
        ckv_past = cache_ckv[l][page_table].reshape(Bd, past_len, KV_RANK)
        kpe_past = cache_kpe[l][page_table].reshape(Bd, past_len, MLA_ROPE)
        xs, b1, b2, b3, b4, b5 = _layer(xs, p_sample[l], pos_s, state_C[l], state_n[l], state_m[l],
                                        (ckv_past, kpe_past), *lw)
        ckv_s.append(b1); kpe_s.append(b2); Cs.append(b3); ns.append(b4); ms.append(b5)

    y_prompt = _rmsnorm(xp, g_final)
    y_sample = _rmsnorm(xs, g_final)
    return (y_prompt, y_sample,
            jnp.stack(ckv_p), jnp.stack(kpe_p), jnp.stack(Cp), jnp.stack(np_), jnp.stack(mp),
            jnp.stack(ckv_s), jnp.stack(kpe_s), jnp.stack(Cs), jnp.stack(ns), jnp.stack(ms))
```

```python
import functools
import math

import jax
import jax.numpy as jnp
import numpy as np
from jax import lax
from jax.experimental import pallas as pl
from jax.experimental.pallas import tpu as pltpu

EPS = 1e-6
ROPE_THETA = 10000.0
F32 = jnp.float32
BF16 = jnp.bfloat16

LANES = 128
SUBLANES = 8
SAMPLE_PAD = SUBLANES
VMEM_LIMIT = 56 * 1024 * 1024
MLSTM_CHUNK = 256
ATTN_TQ = 128
ATTN_TK = 512
PAGES_PER_STEP = 8

NT_DIMS = (((1,), (1,)), ((), ()))
TN_DIMS = (((0,), (0,)), ((), ()))


def _params(*sem):
    return pltpu.CompilerParams(dimension_semantics=sem, vmem_limit_bytes=VMEM_LIMIT)


def _pick_tile(n, prefs):
    for t in prefs:
        if n % t == 0:
            return t
    return n


def _sigmoid(x):
    return 1.0 / (1.0 + jnp.exp(-x))


def _log_sigmoid(x):
    return jnp.minimum(x, 0.0) - jnp.log(1.0 + jnp.exp(-jnp.abs(x)))


def _rms(x, g):
    return x * lax.rsqrt(jnp.mean(x * x, axis=-1, keepdims=True) + EPS) * g


def _norm_matmul_kernel(x_ref, g_ref, w_ref, o_ref, h_ref):
    @pl.when(pl.program_id(1) == 0)
    def _():
        h_ref[...] = _rms(x_ref[...], g_ref[...]).astype(BF16)

    o_ref[...] = jnp.dot(h_ref[...], w_ref[...], preferred_element_type=F32).astype(o_ref.dtype)


def _norm_matmul(x, g, w, *, tm, tn, name):
    m, k = x.shape
    n = w.shape[1]
    return pl.pallas_call(
        _norm_matmul_kernel,
        grid=(m // tm, n // tn),
        in_specs=[pl.BlockSpec((tm, k), lambda i, j: (i, 0)),
                  pl.BlockSpec((1, k), lambda i, j: (0, 0)),
                  pl.BlockSpec((k, tn), lambda i, j: (0, j))],
        out_specs=pl.BlockSpec((tm, tn), lambda i, j: (i, j)),
        out_shape=jax.ShapeDtypeStruct((m, n), F32),
        scratch_shapes=[pltpu.VMEM((tm, k), BF16)],
        compiler_params=_params("parallel", "arbitrary"),
        name=name,
    )(x, g, w)


def _mlstm_kernel(*refs, H, dk, dv, L, t_valid, nc, has_init, gate_lane):
    if has_init:
        (q_ref, k_ref, v_ref, o_ref, z_ref, s_ref, bias_ref, gh_ref, c0_ref, n0_ref, m0_ref,
         h_out, c_out, n_out, m_out, c_s, n_s, m_s) = refs
    else:
        (q_ref, k_ref, v_ref, o_ref, z_ref, s_ref, bias_ref, gh_ref,
         h_out, c_out, n_out, m_out, c_s, n_s, m_s) = refs
    c = pl.program_id(1)

    @pl.when(c == 0)
    def _():
        if has_init:
            c_s[...] = c0_ref[0]
            n_s[...] = n0_ref[0]
            m_s[...] = m0_ref[0]
        else:
            c_s[...] = jnp.zeros_like(c_s)
            n_s[...] = jnp.zeros_like(n_s)
            m_s[...] = jnp.zeros_like(m_s)

    row = lax.broadcasted_iota(jnp.int32, (L, L), 0)
    col = lax.broadcasted_iota(jnp.int32, (L, L), 1)
    causal = col <= row
    eye = col == row
    gates = s_ref[...] + bias_ref[...]
    csum = jnp.dot(causal.astype(F32), _log_sigmoid(gates), preferred_element_type=F32,
                   precision=lax.Precision.HIGHEST)
    tok = lax.broadcasted_iota(jnp.int32, (L, 1), 0)
    scale = dk ** -0.5

    for h in range(H):
        qh = q_ref[:, h * dk:(h + 1) * dk] * scale
        kh = k_ref[:, h * dk:(h + 1) * dk]
        vh = v_ref[:, h * dv:(h + 1) * dv]
        qb = qh.astype(BF16)
        kb = kh.astype(BF16)
        icol = gates[:, gate_lane + h:gate_lane + h + 1]
        bcol = csum[:, gate_lane + H + h:gate_lane + H + h + 1]
        wcol = icol - bcol
        wrow = jnp.sum(jnp.where(eye, wcol, 0.0), axis=0, keepdims=True)
        d = jnp.where(causal, bcol + wrow, -jnp.inf)
        m_prev = m_s[0:1, h:h + 1]
        inter = bcol + m_prev
        m_t = jnp.maximum(inter, jnp.max(d, axis=-1, keepdims=True))
        s_inter = jnp.exp(inter - m_t)
        a = lax.dot_general(qb, kb, NT_DIMS, preferred_element_type=F32) * jnp.exp(d - m_t)
        c_old = c_s[h]
        num = (s_inter * lax.dot_general(qb, c_old.astype(BF16), NT_DIMS, preferred_element_type=F32)
               + jnp.dot(a.astype(BF16), vh.astype(BF16), preferred_element_type=F32))
        n_old = n_s[h:h + 1, :]
        den = (s_inter * jnp.sum(qh * n_old, axis=-1, keepdims=True)
               + jnp.sum(a, axis=-1, keepdims=True))
        hval = num / jnp.maximum(jnp.abs(den), jnp.exp(-m_t))

        m_new = m_t[t_valid - 1:t_valid, :]
        b_last = bcol[t_valid - 1:t_valid, :]
        g_state = jnp.exp(b_last + m_prev - m_new)
        g_tok = jnp.exp(wcol + (b_last - m_new))
        if t_valid < L:
            g_tok = jnp.where(tok < t_valid, g_tok, 0.0)
        vg = (vh * g_tok).astype(BF16)
        c_s[h] = g_state * c_old + lax.dot_general(vg, kb, TN_DIMS, preferred_element_type=F32)
        n_s[h:h + 1, :] = g_state * n_old + jnp.sum(kh * g_tok, axis=0, keepdims=True)
        m_s[0:1, h:h + 1] = m_new

        hg = hval * _sigmoid(o_ref[:, h * dv:(h + 1) * dv])
        hn = _rms(hg, gh_ref[:, h * dv:(h + 1) * dv])
        z = z_ref[:, h * dv:(h + 1) * dv]
        h_out[:, h * dv:(h + 1) * dv] = (hn * (z * _sigmoid(z))).astype(BF16)

    @pl.when(c == nc - 1)
    def _():
        c_out[0] = c_s[...]
        n_out[0] = n_s[...]
        m_out[0] = m_s[...]


def _mlstm(u, us, gate_bias, g_head, init, *, row0, B, S, L, t_valid, H, dk, dv, col, gate_lane, name):
    nc = S // L
    blk0 = row0 // L
    hq, hv = H * dk, H * dv

    def rows(width, off):
        return pl.BlockSpec((L, width), lambda b, c: (blk0 + b * nc + c, off // width))

    in_specs = [rows(hq, col["q"]), rows(hq, col["k"]), rows(hv, col["v"]), rows(hv, col["o"]),
                rows(hv, col["za"]), rows(LANES, 0),
                pl.BlockSpec((1, LANES), lambda b, c: (0, 0)),
                pl.BlockSpec((1, hv), lambda b, c: (0, 0))]
    args = [u, u, u, u, u, us, gate_bias, g_head]
    if init is not None:
        c0, n0, m0 = init
        in_specs += [pl.BlockSpec((1, H, dv, dk), lambda b, c: (b, 0, 0, 0)),
                     pl.BlockSpec((1, H, dk), lambda b, c: (b, 0, 0)),
                     pl.BlockSpec((1, 1, H), lambda b, c: (b, 0, 0))]
        args += [c0, n0, m0.reshape(B, 1, H)]
    kern = functools.partial(_mlstm_kernel, H=H, dk=dk, dv=dv, L=L, t_valid=t_valid, nc=nc,
                             has_init=init is not None, gate_lane=gate_lane)
    h_out, c_out, n_out, m_out = pl.pallas_call(
        kern,
        grid=(B, nc),
        in_specs=in_specs,
        out_specs=[pl.BlockSpec((L, hv), lambda b, c: (b * nc + c, 0)),
                   pl.BlockSpec((1, H, dv, dk), lambda b, c: (b, 0, 0, 0)),
                   pl.BlockSpec((1, H, dk), lambda b, c: (b, 0, 0)),
                   pl.BlockSpec((1, 1, H), lambda b, c: (b, 0, 0))],
        out_shape=[jax.ShapeDtypeStruct((B * S, hv), BF16),
                   jax.ShapeDtypeStruct((B, H, dv, dk), F32),
                   jax.ShapeDtypeStruct((B, H, dk), F32),
                   jax.ShapeDtypeStruct((B, 1, H), F32)],
        scratch_shapes=[pltpu.VMEM((H, dv, dk), F32), pltpu.VMEM((H, dk), F32), pltpu.VMEM((1, H), F32)],
        compiler_params=_params("parallel", "arbitrary"),
        name=name,
    )(*args)
    return h_out, c_out, n_out, m_out.reshape(B, H)


def _rope_lanes(x, cos_t, sin_lo, sin_hi):
    return x * cos_t + pltpu.roll(x, 96, 1) * sin_lo + pltpu.roll(x, 32, 1) * sin_hi


def _mla_prep_kernel(cq_ref, ckv_ref, s_ref, cos_ref, slo_ref, shi_ref, gq_ref, gkv_ref, wq_ref,
                     qn_out, qpe_out, ckv_out, ckvb_out, kpe_out, kpeb_out, *, n_nope, n_pe, rope, scale):
    cos_t, sin_lo, sin_hi = cos_ref[...], slo_ref[...], shi_ref[...]
    cqn = _rms(cq_ref[...], gq_ref[...]).astype(BF16)
    qf = jnp.dot(cqn, wq_ref[...], preferred_element_type=F32)
    qn_out[...] = qf[:, :n_nope].astype(BF16)
    for c in range(n_pe // LANES):
        blk = qf[:, n_nope + c * LANES:n_nope + (c + 1) * LANES]
        qpe_out[:, c * LANES:(c + 1) * LANES] = (_rope_lanes(blk, cos_t, sin_lo, sin_hi) * scale).astype(BF16)
    ckv = _rms(ckv_ref[...], gkv_ref[...])
    ckv_out[...] = ckv
    ckvb_out[...] = ckv.astype(BF16)
    lane = lax.broadcasted_iota(jnp.int32, s_ref.shape, 1)
    kpe = jnp.where(lane < rope, _rope_lanes(s_ref[...], cos_t, sin_lo, sin_hi), 0.0)
    kpe_out[...] = kpe
    kpeb_out[...] = kpe.astype(BF16)


def _mla_prep(u, us, tables, g_q, g_kv, w_q, *, tm, rq, rkv, col, n_nope, n_pe, rope, scale, name):
    rows = u.shape[0]
    cos_t, sin_lo, sin_hi = tables
    tok = lambda width, off: pl.BlockSpec((tm, width), lambda i: (i, off // width))
    const = lambda a: pl.BlockSpec(a.shape, lambda i: (0,) * a.ndim)
    kern = functools.partial(_mla_prep_kernel, n_nope=n_nope, n_pe=n_pe, rope=rope, scale=scale)
    return pl.pallas_call(
        kern,
        grid=(rows // tm,),
        in_specs=[tok(rq, col["cq"]), tok(rkv, col["ckv"]), tok(LANES, 0), tok(LANES, 0), tok(LANES, 0),
                  tok(LANES, 0), const(g_q), const(g_kv), const(w_q)],
        out_specs=[tok(n_nope, 0), tok(n_pe, 0), tok(rkv, 0), tok(rkv, 0), tok(LANES, 0), tok(LANES, 0)],
        out_shape=[jax.ShapeDtypeStruct((rows, n_nope), BF16), jax.ShapeDtypeStruct((rows, n_pe), BF16),
                   jax.ShapeDtypeStruct((rows, rkv), F32), jax.ShapeDtypeStruct((rows, rkv), BF16),
                   jax.ShapeDtypeStruct((rows, LANES), F32), jax.ShapeDtypeStruct((rows, LANES), BF16)],
        compiler_params=_params("parallel"),
        name=name,
    )(u, u, us, cos_t, sin_lo, sin_hi, g_q, g_kv, w_q)


def _attn_prompt_kernel(qn_ref, qpe_ref, ckv_ref, kpe_ref, wuk_ref, wuv_ref, zb_ref, o_ref,
                        qlat_s, m_s, l_s, acc_s, *, H, tq, tk, nope, vdim, scale):
    qi = pl.program_id(1)
    rows = H * tq
    for h in range(H):
        ql = jnp.dot(qn_ref[:, h * nope:(h + 1) * nope], wuk_ref[h], preferred_element_type=F32)
        qlat_s[h * tq:(h + 1) * tq, :] = (ql * scale).astype(BF16)
    qpe = qpe_ref[...].reshape(rows, qpe_ref.shape[-1])
    m_s[...] = jnp.full(m_s.shape, -jnp.inf, F32)
    l_s[...] = jnp.zeros_like(l_s)
    acc_s[...] = jnp.zeros_like(acc_s)

    def step(kb, masked):
        ks = pl.multiple_of(kb * tk, tk)
        kc = ckv_ref[pl.ds(ks, tk), :]
        kp = kpe_ref[pl.ds(ks, tk), :]
        s = (lax.dot_general(qlat_s[...], kc, NT_DIMS, preferred_element_type=F32)
             + lax.dot_general(qpe, kp, NT_DIMS, preferred_element_type=F32))
        if masked:
            q_pos = qi * tq + (lax.broadcasted_iota(jnp.int32, (rows, tk), 0) & (tq - 1))
            k_pos = ks + lax.broadcasted_iota(jnp.int32, (rows, tk), 1)
            s = jnp.where(k_pos <= q_pos, s, -jnp.inf)
        m_old = m_s[...]
        m_new = jnp.maximum(m_old, jnp.max(s, axis=-1, keepdims=True))
        alpha = jnp.exp(m_old - m_new)
        p = jnp.exp(s - m_new)
        l_s[...] = alpha * l_s[...] + jnp.sum(p, axis=-1, keepdims=True)
        acc_s[...] = alpha * acc_s[...] + jnp.dot(p.astype(BF16), kc, preferred_element_type=F32)
        m_s[...] = m_new

    n_full = (qi * tq) // tk

    def body(kb, carry):
        step(kb, False)
        return carry

    lax.fori_loop(0, n_full, body, 0)
    step(n_full, True)

    for h in range(H):
        o_h = acc_s[h * tq:(h + 1) * tq, :] / l_s[h * tq:(h + 1) * tq, :]
        ob = jnp.dot(o_h.astype(BF16), wuv_ref[h], preferred_element_type=F32)
        z = zb_ref[:, h * vdim:(h + 1) * vdim]
        o_ref[:, h * vdim:(h + 1) * vdim] = (ob * (z * _sigmoid(z))).astype(BF16)


def _attn_prompt(qn, qpe_h, ckv_b, kpe_b, wuk, wuv, u, *, B, S, H, tq, tk, col, scale, name):
    nq = S // tq
    nope, rkv = wuk.shape[1], wuk.shape[2]
    vdim = wuv.shape[2]
    rp = qpe_h.shape[-1]
    hv = H * vdim
    kern = functools.partial(_attn_prompt_kernel, H=H, tq=tq, tk=tk, nope=nope, vdim=vdim, scale=scale)
    return pl.pallas_call(
        kern,
        grid=(B, nq),
        in_specs=[pl.BlockSpec((tq, H * nope), lambda b, q: (b * nq + q, 0)),
                  pl.BlockSpec((H, tq, rp), lambda b, q: (0, b * nq + q, 0)),
                  pl.BlockSpec((S, rkv), lambda b, q: (b, 0)),
                  pl.BlockSpec((S, rp), lambda b, q: (b, 0)),
                  pl.BlockSpec(wuk.shape, lambda b, q: (0, 0, 0)),
                  pl.BlockSpec(wuv.shape, lambda b, q: (0, 0, 0)),
                  pl.BlockSpec((tq, hv), lambda b, q: (b * nq + q, col["zb"] // hv))],
        out_specs=pl.BlockSpec((tq, hv), lambda b, q: (b * nq + q, 0)),
        out_shape=jax.ShapeDtypeStruct((B * S, hv), BF16),
        scratch_shapes=[pltpu.VMEM((H * tq, rkv), BF16), pltpu.VMEM((H * tq, 1), F32),
                        pltpu.VMEM((H * tq, 1), F32), pltpu.VMEM((H * tq, rkv), F32)],
        compiler_params=_params("parallel", "arbitrary"),
        name=name,
    )(qn, qpe_h, ckv_b, kpe_b, wuk, wuv, u)


def _qlat_kernel(qn_ref, wuk_ref, o_ref, *, scale):
    o_ref[0] = (jnp.dot(qn_ref[...], wuk_ref[0], preferred_element_type=F32) * scale).astype(BF16)


def _qlat(qn, wuk, *, row0, rows, scale, name):
    H, nope, rkv = wuk.shape
    return pl.pallas_call(
        functools.partial(_qlat_kernel, scale=scale),
        grid=(H,),
        in_specs=[pl.BlockSpec((rows, nope), lambda h: (row0 // rows, h)),
                  pl.BlockSpec((1, nope, rkv), lambda h: (h, 0, 0))],
        out_specs=pl.BlockSpec((1, rows, rkv), lambda h: (h, 0, 0)),
        out_shape=jax.ShapeDtypeStruct((H, rows, rkv), BF16),
        compiler_params=_params("parallel"),
        name=name,
    )(qn, wuk)


def _ouv_kernel(ol_ref, wuv_ref, zb_ref, o_ref):
    ob = jnp.dot(ol_ref[0], wuv_ref[0], preferred_element_type=F32)
    z = zb_ref[...]
    o_ref[...] = (ob * (z * _sigmoid(z))).astype(BF16)


def _ouv(o_lat, wuv, u, *, row0, rows, zb_col, name):
    H, rkv, vdim = wuv.shape
    return pl.pallas_call(
        _ouv_kernel,
        grid=(H,),
        in_specs=[pl.BlockSpec((1, rows, rkv), lambda h: (h, 0, 0)),
                  pl.BlockSpec((1, rkv, vdim), lambda h: (h, 0, 0)),
                  pl.BlockSpec((rows, vdim), lambda h: (row0 // rows, zb_col // vdim + h))],
        out_specs=pl.BlockSpec((rows, vdim), lambda h: (0, h)),
        out_shape=jax.ShapeDtypeStruct((rows, H * vdim), BF16),
        compiler_params=_params("parallel"),
        name=name,
    )(o_lat, wuv, u)


def _attn_sample_kernel(pt_ref, ql_ref, qp_ref, cn_ref, kn_ref, *rest, P, T, n_steps):
    ckv_pages = rest[:P]
    kpe_pages = rest[P:2 * P]
    o_ref, m_s, l_s, acc_s = rest[2 * P:]
    j = pl.program_id(1)
    ql = ql_ref[0]
    qp = qp_ref[0]

    @pl.when(j == 0)
    def _():
        m_s[...] = jnp.full(m_s.shape, -jnp.inf, F32)
        l_s[...] = jnp.zeros_like(l_s)
        acc_s[...] = jnp.zeros_like(acc_s)

    def update(s, v):
        m_old = m_s[...]
        m_new = jnp.maximum(m_old, jnp.max(s, axis=-1, keepdims=True))
        alpha = jnp.exp(m_old - m_new)
        p = jnp.exp(s - m_new)
        l_s[...] = alpha * l_s[...] + jnp.sum(p, axis=-1, keepdims=True)
        acc_s[...] = alpha * acc_s[...] + jnp.dot(p.astype(BF16), v, preferred_element_type=F32)
        m_s[...] = m_new

    kc = jnp.concatenate([r[0, 0] for r in ckv_pages], axis=0).astype(BF16)
    kp = jnp.concatenate([r[0, 0] for r in kpe_pages], axis=0).astype(BF16)
    update(lax.dot_general(ql, kc, NT_DIMS, preferred_element_type=F32)
           + lax.dot_general(qp, kp, NT_DIMS, preferred_element_type=F32), kc)

    @pl.when(j == n_steps - 1)
    def _():
        cn = cn_ref[0]
        kn = kn_ref[0]
        s = (lax.dot_general(ql, cn, NT_DIMS, preferred_element_type=F32)
             + lax.dot_general(qp, kn, NT_DIMS, preferred_element_type=F32))
        t_q = lax.broadcasted_iota(jnp.int32, s.shape, 0) & (T - 1)
        t_k = lax.broadcasted_iota(jnp.int32, s.shape, 1)
        update(jnp.where(t_k <= t_q, s, -jnp.inf), cn)
        o_ref[0] = (acc_s[...] / l_s[...]).astype(BF16)


def _attn_sample(page_table, ql, qp, cn, kn, cache_ckv, cache_kpe, *, layer, T, P, name):
    Bd, rows, rkv = ql.shape
    rope = qp.shape[-1]
    n_pages = page_table.shape[1]
    page = cache_ckv.shape[2]
    n_steps = n_pages // P
    assert T & (T - 1) == 0 and n_pages % P == 0

    def page_spec(width, i):
        return pl.BlockSpec((1, 1, page, width),
                            lambda b, j, pt: (layer, pt[b * n_pages + j * P + i], 0, 0))

    per_b = lambda shape: pl.BlockSpec((1,) + shape, lambda b, j, pt: (b, 0, 0))
    grid_spec = pltpu.PrefetchScalarGridSpec(
        num_scalar_prefetch=1,
        grid=(Bd, n_steps),
        in_specs=([per_b((rows, rkv)), per_b((rows, rope)), per_b((SAMPLE_PAD, rkv)), per_b((SAMPLE_PAD, rope))]
                  + [page_spec(rkv, i) for i in range(P)] + [page_spec(rope, i) for i in range(P)]),
        out_specs=per_b((rows, rkv)),
        scratch_shapes=[pltpu.VMEM((rows, 1), F32), pltpu.VMEM((rows, 1), F32), pltpu.VMEM((rows, rkv), F32)],
    )
    kern = functools.partial(_attn_sample_kernel, P=P, T=T, n_steps=n_steps)
    return pl.pallas_call(
        kern,
        grid_spec=grid_spec,
        out_shape=jax.ShapeDtypeStruct((Bd, rows, rkv), BF16),
        compiler_params=_params("parallel", "arbitrary"),
        name=name,
    )(page_table.reshape(-1), ql, qp, cn, kn, *([cache_ckv] * P), *([cache_kpe] * P))


def _mix_kernel(ha_ref, ob_ref, wpa_ref, wpb_ref, ga_ref, gb_ref, o_ref):
    ya = jnp.dot(ha_ref[...], wpa_ref[...], preferred_element_type=F32)
    yb = jnp.dot(ob_ref[...], wpb_ref[...], preferred_element_type=F32)
    o_ref[...] = (_sigmoid(ga_ref[...]) * ya + _sigmoid(gb_ref[...]) * yb).astype(BF16)


def _mix(ha, ob, wpa, wpb, u, *, tm, tn, col, name):
    rows, ka = ha.shape
    kb = ob.shape[1]
    n = wpa.shape[1]
    return pl.pallas_call(
        _mix_kernel,
        grid=(rows // tm, n // tn),
        in_specs=[pl.BlockSpec((tm, ka), lambda i, j: (i, 0)),
                  pl.BlockSpec((tm, kb), lambda i, j: (i, 0)),
                  pl.BlockSpec((ka, tn), lambda i, j: (0, j)),
                  pl.BlockSpec((kb, tn), lambda i, j: (0, j)),
                  pl.BlockSpec((tm, tn), lambda i, j: (i, col["ga"] // tn + j)),
                  pl.BlockSpec((tm, tn), lambda i, j: (i, col["gb"] // tn + j))],
        out_specs=pl.BlockSpec((tm, tn), lambda i, j: (i, j)),
        out_shape=jax.ShapeDtypeStruct((rows, n), BF16),
        compiler_params=_params("parallel", "arbitrary"),
        name=name,
    )(ha, ob, wpa, wpb, u, u)


def _out_proj_kernel(mix_ref, w_ref, x_ref, o_ref):
    o_ref[...] = x_ref[...] + jnp.dot(mix_ref[...], w_ref[...], preferred_element_type=F32)


def _out_proj(mix, w, x, *, tm, tn, name):
    rows, k = mix.shape
    n = w.shape[1]
    return pl.pallas_call(
        _out_proj_kernel,
        grid=(rows // tm, n // tn),
        in_specs=[pl.BlockSpec((tm, k), lambda i, j: (i, 0)),
                  pl.BlockSpec((k, tn), lambda i, j: (0, j)),
                  pl.BlockSpec((tm, tn), lambda i, j: (i, j))],
        out_specs=pl.BlockSpec((tm, tn), lambda i, j: (i, j)),
        out_shape=jax.ShapeDtypeStruct((rows, n), F32),
        compiler_params=_params("parallel", "arbitrary"),
        name=name,
    )(mix, w, x)


def _ple_kernel(xf_ref, g_ref, wpg_ref, p_ref, wple_ref, x_ref, o_ref, h_ref):
    @pl.when(pl.program_id(1) == 0)
    def _():
        h_ref[...] = _rms(xf_ref[...], g_ref[...]).astype(BF16)

    gate = _sigmoid(jnp.dot(h_ref[...], wpg_ref[...], preferred_element_type=F32))
    emb = jnp.dot(p_ref[...].astype(BF16), wple_ref[...], preferred_element_type=F32)
    o_ref[...] = x_ref[...] + emb * gate


def _ple(x, g, wpg, p, wple, *, tm, tn, name):
    rows, d = x.shape
    n = wpg.shape[1]
    pd = p.shape[1]
    return pl.pallas_call(
        _ple_kernel,
        grid=(rows // tm, n // tn),
        in_specs=[pl.BlockSpec((tm, d), lambda i, j: (i, 0)),
                  pl.BlockSpec((1, d), lambda i, j: (0, 0)),
                  pl.BlockSpec((d, tn), lambda i, j: (0, j)),
                  pl.BlockSpec((tm, pd), lambda i, j: (i, 0)),
                  pl.BlockSpec((pd, tn), lambda i, j: (0, j)),
                  pl.BlockSpec((tm, tn), lambda i, j: (i, j))],
        out_specs=pl.BlockSpec((tm, tn), lambda i, j: (i, j)),
        out_shape=jax.ShapeDtypeStruct((rows, n), F32),
        scratch_shapes=[pltpu.VMEM((tm, d), BF16)],
        compiler_params=_params("parallel", "arbitrary"),
        name=name,
    )(x, g, wpg, p, wple, x)


def _final_norm_kernel(x_ref, g_ref, o_ref):
    o_ref[...] = _rms(x_ref[...], g_ref[...])


def _final_norm(x, g, *, tm, name):
    rows, d = x.shape
    return pl.pallas_call(
        _final_norm_kernel,
        grid=(rows // tm,),
        in_specs=[pl.BlockSpec((tm, d), lambda i: (i, 0)), pl.BlockSpec((1, d), lambda i: (0, 0))],
        out_specs=pl.BlockSpec((tm, d), lambda i: (i, 0)),
        out_shape=jax.ShapeDtypeStruct((rows, d), F32),
        compiler_params=_params("parallel"),
        name=name,
    )(x, g)


def _rope_tables(pos, rope):
    half = rope // 2
    inv = ROPE_THETA ** (-jnp.arange(half, dtype=F32) / half)
    ang = pos[:, None] * inv[None, :]
    cos, sin, zero = jnp.cos(ang), jnp.sin(ang), jnp.zeros_like(ang)
    reps = LANES // rope
    cos_t = jnp.tile(jnp.concatenate([cos, cos], axis=1), (1, reps))
    sin_lo = jnp.tile(jnp.concatenate([-sin, zero], axis=1), (1, reps))
    sin_hi = jnp.tile(jnp.concatenate([zero, sin], axis=1), (1, reps))
    return cos_t, sin_lo, sin_hi


def kernel(x_prompt, x_sample, p_prompt, p_sample, cache_ckv, cache_kpe, state_C, state_n, state_m,
           page_table, g_in, w_in, b_i, b_f, g_q, w_uq, g_kv, w_uk, w_uv, g_head, w_pa, w_pb,
           w_out, w_ple, g_ple, w_pg, g_final):
    B, S, D = x_prompt.shape
    Bd, T, _ = x_sample.shape
    depth = g_in.shape[0]
    H = b_i.shape[1]
    dv = D // H
    dk = dv // 2
    hq, hv = H * dk, H * dv
    rq, rkv = g_q.shape[1], g_kv.shape[1]
    HB = w_uk.shape[2]
    nope, vdim = w_uk.shape[3], w_uv.shape[3]
    rope = w_uq.shape[3] - nope
    wb = HB * vdim
    page = cache_ckv.shape[2]
    past_len = page_table.shape[1] * page
    mla_scale = float((nope + rope) ** -0.5)
    assert T <= SAMPLE_PAD and LANES % rope == 0 and rope + 2 * H <= LANES
    assert hv == D and wb == D and rq == rkv and (7 * D) % rq == 0

    n_p, n_s = B * S, Bd * SAMPLE_PAD
    rows = n_p + n_s
    tm = _pick_tile(math.gcd(n_p, n_s), (512, 256, 128, 64, 32, 16, 8))

    pad_t = ((0, 0), (0, SAMPLE_PAD - T), (0, 0))
    x = jnp.concatenate([x_prompt.reshape(n_p, D), jnp.pad(x_sample, pad_t).reshape(n_s, D)], axis=0)
    pos = jnp.concatenate([jnp.tile(jnp.arange(S, dtype=F32), B),
                           jnp.tile(past_len + jnp.arange(SAMPLE_PAD, dtype=F32), Bd)])
    tables = _rope_tables(pos, rope)

    col = {"v": 0, "o": D, "za": 2 * D, "zb": 3 * D, "ga": 4 * D, "gb": 5 * D,
           "q": 6 * D, "k": 6 * D + hq, "cq": 7 * D, "ckv": 7 * D + rq}
    n_main = 7 * D + rq + rkv
    sizes = (hq, hq, hv, H, H, hv, hv, rq, rkv, rope, wb, D, D)
    starts = np.concatenate([[0], np.cumsum(sizes)])
    seg = {name: (int(starts[i]), int(starts[i + 1])) for i, name in enumerate(
        ("q", "k", "v", "i", "f", "o", "za", "cq", "ckv", "kr", "zb", "ga", "gb"))}
    main_order = ("v", "o", "za", "zb", "ga", "gb", "q", "k", "cq", "ckv")
    gate_lane = rope
    tn_main = _pick_tile(n_main, (1024, 512, 256, 128))
    tn_d = _pick_tile(D, (1024, 512, 256, 128))

    xs = x
    outs = {k: [] for k in ("ckv_p", "kpe_p", "C_p", "n_p", "m_p", "ckv_s", "kpe_s", "C_s", "n_s", "m_s")}
    for l in range(depth):
        wl = w_in[l]
        w_main = jnp.concatenate([wl[:, seg[n][0]:seg[n][1]] for n in main_order], axis=1).astype(BF16)
        w_small = jnp.concatenate(
            [wl[:, seg[n][0]:seg[n][1]] for n in ("kr", "i", "f")]
            + [jnp.zeros((D, LANES - rope - 2 * H), F32)], axis=1).astype(BF16)
        gate_bias = jnp.concatenate([jnp.zeros((rope,), F32), b_i[l], b_f[l],
                                     jnp.zeros((LANES - rope - 2 * H,), F32)]).reshape(1, LANES)
        wq = w_uq[l]
        w_q = jnp.concatenate([wq[:, :, :nope].reshape(rq, HB * nope),
                               wq[:, :, nope:].reshape(rq, HB * rope)], axis=1).astype(BF16)
        wuk = jnp.transpose(w_uk[l], (1, 2, 0)).astype(BF16)
        wuv = jnp.transpose(w_uv[l], (1, 0, 2)).astype(BF16)

        g_in_l = g_in[l].reshape(1, D)
        u = _norm_matmul(xs, g_in_l, w_main, tm=tm, tn=tn_main, name=f"inproj_main_{l}")
        us = _norm_matmul(xs, g_in_l, w_small, tm=tm, tn=LANES, name=f"inproj_small_{l}")

        g_head_l = g_head[l].reshape(1, hv)
        Lp = _pick_tile(S, (MLSTM_CHUNK, 128, 64, 32, 16, 8))
        ha_p, C_p, nn_p, m_p = _mlstm(u, us, gate_bias, g_head_l, None, row0=0, B=B, S=S, L=Lp, t_valid=Lp,
                                      H=H, dk=dk, dv=dv, col=col, gate_lane=gate_lane, name=f"mlstm_prompt_{l}")
        ha_s, C_s, nn_s, m_s = _mlstm(u, us, gate_bias, g_head_l, (state_C[l], state_n[l], state_m[l]),
                                      row0=n_p, B=Bd, S=SAMPLE_PAD, L=SAMPLE_PAD, t_valid=T,
                                      H=H, dk=dk, dv=dv, col=col, gate_lane=gate_lane, name=f"mlstm_sample_{l}")
        ha = jnp.concatenate([ha_p, ha_s], axis=0)

        qn, qpe, ckv, ckv_b, kpe, kpe_b = _mla_prep(
            u, us, tables, g_q[l].reshape(1, rq), g_kv[l].reshape(1, rkv), w_q, tm=tm, rq=rq, rkv=rkv,
            col=col, n_nope=HB * nope, n_pe=HB * rope, rope=rope, scale=mla_scale, name=f"mla_prep_{l}")
        kpe_b = kpe_b[:, :rope]
        qpe_h = jnp.transpose(qpe.reshape(rows, HB, rope), (1, 0, 2))
        tq = _pick_tile(S, (ATTN_TQ, 64, 32, 16, 8))
        tk = _pick_tile(S, (ATTN_TK, 256, 128))
        ob_p = _attn_prompt(qn, qpe_h, ckv_b, kpe_b, wuk, wuv, u, B=B, S=S, H=HB, tq=tq, tk=tk, col=col,
                            scale=mla_scale, name=f"attn_prompt_{l}")

        ql_s = _qlat(qn, wuk, row0=n_p, rows=n_s, scale=mla_scale, name=f"qlat_sample_{l}")
        to_bt = lambda a: jnp.transpose(a.reshape(HB, Bd, SAMPLE_PAD, -1)[:, :, :T], (1, 0, 2, 3)).reshape(
            Bd, HB * T, a.shape[-1])
        ol_s = _attn_sample(page_table, to_bt(ql_s), to_bt(qpe_h[:, n_p:]),
                            ckv_b[n_p:].reshape(Bd, SAMPLE_PAD, rkv), kpe_b[n_p:].reshape(Bd, SAMPLE_PAD, rope),
                            cache_ckv, cache_kpe, layer=l, T=T,
                            P=_pick_tile(page_table.shape[1], (PAGES_PER_STEP, 4, 2, 1)), name=f"attn_sample_{l}")
        ol_h = jnp.pad(jnp.transpose(ol_s.reshape(Bd, HB, T, rkv), (1, 0, 2, 3)),
                       ((0, 0), (0, 0), (0, SAMPLE_PAD - T), (0, 0))).reshape(HB, n_s, rkv)
        ob_s = _ouv(ol_h, wuv, u, row0=n_p, rows=n_s, zb_col=col["zb"], name=f"ouv_sample_{l}")
        ob = jnp.concatenate([ob_p, ob_s], axis=0)

        mix = _mix(ha, ob, w_pa[l].astype(BF16), w_pb[l].astype(BF16), u, tm=tm, tn=tn_d, col=col, name=f"mix_{l}")
        x1 = _out_proj(mix, w_out[l].astype(BF16), xs, tm=tm, tn=tn_d, name=f"out_proj_{l}")
        p_all = jnp.concatenate([p_prompt[l].reshape(n_p, -1),
                                 jnp.pad(p_sample[l], pad_t).reshape(n_s, -1)], axis=0)
        xs = _ple(x1, g_ple[l].reshape(1, D), w_pg[l].astype(BF16), p_all, w_ple[l].astype(BF16),
                  tm=tm, tn=tn_d, name=f"ple_{l}")

        unpad = lambda a: a[n_p:].reshape(Bd, SAMPLE_PAD, -1)[:, :T]
        outs["ckv_p"].append(ckv[:n_p].reshape(B, S, rkv))
        outs["kpe_p"].append(kpe[:n_p, :rope].reshape(B, S, rope))
        outs["C_p"].append(C_p); outs["n_p"].append(nn_p); outs["m_p"].append(m_p)
        outs["ckv_s"].append(unpad(ckv))
        outs["kpe_s"].append(unpad(kpe)[:, :, :rope])
        outs["C_s"].append(C_s); outs["n_s"].append(nn_s); outs["m_s"].append(m_s)

    y = _final_norm(xs, g_final.reshape(1, D), tm=tm, name="final_norm")
    y_prompt = y[:n_p].reshape(B, S, D)
    y_sample = y[n_p:].reshape(Bd, SAMPLE_PAD, D)[:, :T]
    st = lambda k: jnp.stack(outs[k])
    return (y_prompt, y_sample, st("ckv_p"), st("kpe_p"), st("C_p"), st("n_p"), st("m_p"),
            st("ckv_s"), st("kpe_s"), st("C_s"), st("n_s"), st("m_s"))
```

```python
import functools
import math

import jax
import jax.numpy as jnp
import numpy as np
from jax import lax
from jax.experimental import pallas as pl
from jax.experimental.pallas import tpu as pltpu

EPS = 1e-6
ROPE_THETA = 10000.0
F32 = jnp.float32
BF16 = jnp.bfloat16

LANES = 128
SUBLANES = 8
MXU_COLS = 256
SAMPLE_PAD = SUBLANES
VMEM_LIMIT = 56 * 1024 * 1024
MLSTM_CHUNK = 256
MLSTM_SEQS_PER_STEP = 4
ATTN_TQ = 128
ATTN_TK = 512
ATTN_CHUNK = 2 * MXU_COLS
PAGES_PER_STEP = 32
INPROJ_TM = 1024

NT_DIMS = (((1,), (1,)), ((), ()))
TN_DIMS = (((0,), (0,)), ((), ()))


def _params(*sem):
    return pltpu.CompilerParams(dimension_semantics=sem, vmem_limit_bytes=VMEM_LIMIT)


def _pick_tile(n, prefs):
    for t in prefs:
        if n % t == 0:
            return t
    return n


def _sigmoid(x):
    return 1.0 / (1.0 + jnp.exp(-x))


def _log_sigmoid(x):
    return jnp.minimum(x, 0.0) - jnp.log(1.0 + jnp.exp(-jnp.abs(x)))


def _rms(x, g):
    return x * lax.rsqrt(jnp.mean(x * x, axis=-1, keepdims=True) + EPS) * g


_ANY = pl.BlockSpec(memory_space=pl.ANY)


def _norm_matmul_kernel(x_ref, g_ref, w_ref, o_ref, h_ref):
    @pl.when(pl.program_id(1) == 0)
    def _():
        h_ref[...] = _rms(x_ref[...], g_ref[...]).astype(BF16)

    o_ref[...] = jnp.dot(h_ref[...], w_ref[...], preferred_element_type=F32).astype(o_ref.dtype)


def _norm_matmul(x, g, w, *, tm, tn, out_dtype, name):
    m, k = x.shape
    n = w.shape[1]
    return pl.pallas_call(
        _norm_matmul_kernel,
        grid=(m // tm, n // tn),
        in_specs=[pl.BlockSpec((tm, k), lambda i, j: (i, 0)),
                  pl.BlockSpec((1, k), lambda i, j: (0, 0)),
                  pl.BlockSpec((k, tn), lambda i, j: (0, j))],
        out_specs=pl.BlockSpec((tm, tn), lambda i, j: (i, j)),
        out_shape=jax.ShapeDtypeStruct((m, n), out_dtype),
        scratch_shapes=[pltpu.VMEM((tm, k), BF16)],
        compiler_params=_params("parallel", "arbitrary"),
        name=name,
    )(x, g, w)


def _mlstm_kernel(*refs, H, dk, dv, L, BB, t_valid, nc, has_init, n_alias, gate_lane):
    q_ref, k_ref, v_ref, o_ref, z_ref, s_ref, bias_ref, gh_ref = refs[:8]
    pos = 8
    if has_init:
        c0_ref, n0_ref, m0_ref = refs[pos:pos + 3]
        pos += 3
    pos += n_alias
    h_out, c_out, n_out, m_out, c_s, n_s, m_s = refs[pos:]
    c = pl.program_id(1)

    @pl.when(c == 0)
    def _():
        if has_init:
            c_s[...] = c0_ref[0]
            n_s[...] = n0_ref[0]
            m_s[...] = m0_ref[0]
        else:
            c_s[...] = jnp.zeros_like(c_s)
            n_s[...] = jnp.zeros_like(n_s)
            m_s[...] = jnp.zeros_like(m_s)

    row = lax.broadcasted_iota(jnp.int32, (L, L), 0)
    col = lax.broadcasted_iota(jnp.int32, (L, L), 1)
    causal = col <= row
    eye = col == row
    tok = lax.broadcasted_iota(jnp.int32, (L, 1), 0)
    head_row = lax.broadcasted_iota(jnp.int32, (H, dk), 0)
    head_lane = lax.broadcasted_iota(jnp.int32, (1, H), 1)
    scale = dk ** -0.5

    for bb, h in [(bb, h) for bb in range(BB) for h in range(H)]:
        r = slice(bb * L, (bb + 1) * L)
        if h == 0:
            n_all, m_all = n_s[bb], m_s[bb]
            n_acc, m_acc = n_all, m_all
            gates = s_ref[r, :] + bias_ref[...]
            csum = jnp.dot(causal.astype(F32), _log_sigmoid(gates), preferred_element_type=F32,
                           precision=lax.Precision.HIGHEST)
        qb = q_ref[r, h * dk:(h + 1) * dk]
        kb = k_ref[r, h * dk:(h + 1) * dk]
        vb = v_ref[r, h * dv:(h + 1) * dv]
        icol = gates[:, gate_lane + h:gate_lane + h + 1]
        bcol = csum[:, gate_lane + H + h:gate_lane + H + h + 1]
        wcol = icol - bcol
        wrow = jnp.sum(jnp.where(eye, wcol, 0.0), axis=0, keepdims=True)
        d = jnp.where(causal, bcol + wrow, -jnp.inf)
        m_prev = m_all[:, h:h + 1]
        inter = bcol + m_prev
        m_t = jnp.maximum(inter, jnp.max(d, axis=-1, keepdims=True))
        s_inter = jnp.exp(inter - m_t) * scale
        a = lax.dot_general(qb, kb, NT_DIMS, preferred_element_type=F32) * (jnp.exp(d - m_t) * scale)
        c_old = c_s[bb, h]
        num = (s_inter * lax.dot_general(qb, c_old.astype(BF16), NT_DIMS, preferred_element_type=F32)
               + jnp.dot(a.astype(BF16), vb, preferred_element_type=F32))
        n_old = n_all[h:h + 1, :]
        den = (s_inter * jnp.sum(qb.astype(F32) * n_old, axis=-1, keepdims=True)
               + jnp.sum(a, axis=-1, keepdims=True))
        hval = num / jnp.maximum(jnp.abs(den), jnp.exp(-m_t))

        m_new = m_t[t_valid - 1:t_valid, :]
        b_last = bcol[t_valid - 1:t_valid, :]
        g_state = jnp.exp(b_last + m_prev - m_new)
        g_tok = jnp.exp(wcol + (b_last - m_new))
        if t_valid < L:
            g_tok = jnp.where(tok < t_valid, g_tok, 0.0)
        vg = (vb.astype(F32) * g_tok).astype(BF16)
        c_s[bb, h] = g_state * c_old + lax.dot_general(vg, kb, TN_DIMS, preferred_element_type=F32)
        n_new = g_state * n_old + jnp.sum(kb.astype(F32) * g_tok, axis=0, keepdims=True)
        n_acc = jnp.where(head_row == h, n_new, n_acc)
        m_acc = jnp.where(head_lane == h, m_new, m_acc)
        if h == H - 1:
            n_s[bb] = n_acc
            m_s[bb] = m_acc

        hg = hval * _sigmoid(o_ref[r, h * dv:(h + 1) * dv].astype(F32))
        hn = _rms(hg, gh_ref[:, h * dv:(h + 1) * dv])
        z = z_ref[r, h * dv:(h + 1) * dv].astype(F32)
        h_out[r, h * dv:(h + 1) * dv] = (hn * (z * _sigmoid(z))).astype(BF16)

    @pl.when(c == nc - 1)
    def _():
        c_out[0] = c_s[...]
        n_out[...] = n_s[...]
        m_out[...] = m_s[...]


def _mlstm(u, us, gate_bias, g_head, init, h_buf, c_buf, *, layer, c_layers, rows_total, row0, B, S, L, t_valid,
           H, dk, dv, col, gate_lane, name):
    nc = S // L
    BB = _pick_tile(B, (MLSTM_SEQS_PER_STEP, 2, 1)) if nc == 1 else 1
    rb = BB * L
    assert row0 % rb == 0
    blk0 = row0 // rb
    hq, hv = H * dk, H * dv

    def rows(width, off):
        return pl.BlockSpec((rb, width), lambda b, c: (blk0 + b * nc + c, off // width))

    in_specs = [rows(hq, col["q"]), rows(hq, col["k"]), rows(hv, col["v"]), rows(hv, col["o"]),
                rows(hv, col["za"]), rows(LANES, 0),
                pl.BlockSpec((1, LANES), lambda b, c: (0, 0)),
                pl.BlockSpec((1, hv), lambda b, c: (0, 0))]
    args = [u, u, u, u, u, us, gate_bias, g_head]
    if init is not None:
        c0, n0, m0 = init
        in_specs += [pl.BlockSpec((1, BB, H, dv, dk), lambda b, c: (layer, b, 0, 0, 0)),
                     pl.BlockSpec((1, BB, H, dk), lambda b, c: (layer, b, 0, 0)),
                     pl.BlockSpec((1, BB, 1, H), lambda b, c: (layer, b, 0, 0))]
        args += [c0, n0, m0.reshape(m0.shape[0], B, 1, H)]
    aliases = {}
    for buf, out_idx in ((h_buf, 0), (c_buf, 1)):
        if buf is not None:
            aliases[len(args)] = out_idx
            in_specs.append(_ANY)
            args.append(buf)
    kern = functools.partial(_mlstm_kernel, H=H, dk=dk, dv=dv, L=L, BB=BB, t_valid=t_valid, nc=nc,
                             has_init=init is not None, n_alias=len(aliases), gate_lane=gate_lane)
    h_out, c_out, n_out, m_out = pl.pallas_call(
        kern,
        grid=(B // BB, nc),
        in_specs=in_specs,
        out_specs=[pl.BlockSpec((rb, hv), lambda b, c: (blk0 + b * nc + c, 0)),
                   pl.BlockSpec((1, BB, H, dv, dk), lambda b, c: (layer, b, 0, 0, 0)),
                   pl.BlockSpec((BB, H, dk), lambda b, c: (b, 0, 0)),
                   pl.BlockSpec((BB, 1, H), lambda b, c: (b, 0, 0))],
        out_shape=[jax.ShapeDtypeStruct((rows_total, hv), BF16),
                   jax.ShapeDtypeStruct((c_layers, B, H, dv, dk), F32),
                   jax.ShapeDtypeStruct((B, H, dk), F32),
                   jax.ShapeDtypeStruct((B, 1, H), F32)],
        scratch_shapes=[pltpu.VMEM((BB, H, dv, dk), F32), pltpu.VMEM((BB, H, dk), F32),
                        pltpu.VMEM((BB, 1, H), F32)],
        input_output_aliases=aliases,
        compiler_params=_params("parallel", "arbitrary"),
        name=name,
    )(*args)
    return h_out, c_out, n_out, m_out.reshape(B, H)


def _rope_lanes(x, cos_t, sin_lo, sin_hi):
    return x * cos_t + pltpu.roll(x, 96, 1) * sin_lo + pltpu.roll(x, 32, 1) * sin_hi


def _mla_prep_kernel(cq_ref, ckv_ref, s_ref, cos_ref, slo_ref, shi_ref, gq_ref, gkv_ref, wq_ref,
                     qn_out, qpe_out, ckv_out, ckvb_out, kpe_out, kpeb_out, *, n_nope, n_pe, rope, scale):
    cos_t, sin_lo, sin_hi = cos_ref[...], slo_ref[...], shi_ref[...]
    cqn = _rms(cq_ref[...].astype(F32), gq_ref[...]).astype(BF16)
    qf = jnp.dot(cqn, wq_ref[...], preferred_element_type=F32)
    qn_out[...] = qf[:, :n_nope].astype(BF16)
    for c in range(n_pe // LANES):
        blk = qf[:, n_nope + c * LANES:n_nope + (c + 1) * LANES]
        qpe_out[:, c * LANES:(c + 1) * LANES] = (_rope_lanes(blk, cos_t, sin_lo, sin_hi) * scale).astype(BF16)
    ckv = _rms(ckv_ref[...].astype(F32), gkv_ref[...])
    ckv_out[...] = ckv
    ckvb_out[...] = ckv.astype(BF16)
    lane = lax.broadcasted_iota(jnp.int32, s_ref.shape, 1)
    kpe = jnp.where(lane < rope, _rope_lanes(s_ref[...], cos_t, sin_lo, sin_hi), 0.0)
    kpe_out[...] = kpe
    kpeb_out[...] = kpe.astype(BF16)


def _mla_prep(u, us, tables, g_q, g_kv, w_q, *, tm, rq, rkv, col, n_nope, n_pe, rope, scale, name):
    rows = u.shape[0]
    cos_t, sin_lo, sin_hi = tables
    tok = lambda width, off: pl.BlockSpec((tm, width), lambda i: (i, off // width))
    const = lambda a: pl.BlockSpec(a.shape, lambda i: (0,) * a.ndim)
    kern = functools.partial(_mla_prep_kernel, n_nope=n_nope, n_pe=n_pe, rope=rope, scale=scale)
    return pl.pallas_call(
        kern,
        grid=(rows // tm,),
        in_specs=[tok(rq, col["cq"]), tok(rkv, col["ckv"]), tok(LANES, 0), tok(LANES, 0), tok(LANES, 0),
                  tok(LANES, 0), const(g_q), const(g_kv), const(w_q)],
        out_specs=[tok(n_nope, 0), tok(n_pe, 0), tok(rkv, 0), tok(rkv, 0), tok(LANES, 0), tok(LANES, 0)],
        out_shape=[jax.ShapeDtypeStruct((rows, n_nope), BF16), jax.ShapeDtypeStruct((rows, n_pe), BF16),
                   jax.ShapeDtypeStruct((rows, rkv), F32), jax.ShapeDtypeStruct((rows, rkv), BF16),
                   jax.ShapeDtypeStruct((rows, LANES), F32), jax.ShapeDtypeStruct((rows, LANES), BF16)],
        compiler_params=_params("parallel"),
        name=name,
    )(u, u, us, cos_t, sin_lo, sin_hi, g_q, g_kv, w_q)


def _attn_prompt_kernel(qn_ref, qpe_ref, ckv_ref, ckvt_ref, kpe_ref, wuk_ref, wuv_ref, zb_ref, o_ref,
                        qlat_s, m_s, l_s, acc_s, *, H, tq, tk, nope, vdim, scale, chunk):
    qi = pl.program_id(1)
    cols = H * tq
    for h in range(H):
        ql = jnp.dot(qn_ref[:, h * nope:(h + 1) * nope], wuk_ref[h], preferred_element_type=F32)
        qlat_s[h * tq:(h + 1) * tq, :] = (ql * scale).astype(BF16)
    m_s[...] = jnp.full(m_s.shape, -jnp.inf, F32)
    l_s[...] = jnp.zeros_like(l_s)
    acc_s[...] = jnp.zeros_like(acc_s)

    def step(kb, masked):
        ks = pl.multiple_of(kb * tk, tk)
        kc = ckv_ref[pl.ds(ks, tk), :]
        kp = kpe_ref[pl.ds(ks, tk), :]
        kct = ckvt_ref[kb]
        for c0 in range(0, cols, chunk):
            sl = slice(c0, c0 + chunk)
            qp = qpe_ref[c0 // tq:(c0 + chunk) // tq].reshape(chunk, qpe_ref.shape[-1])
            s = (lax.dot_general(kc, qlat_s[sl, :], NT_DIMS, preferred_element_type=F32)
                 + lax.dot_general(kp, qp, NT_DIMS, preferred_element_type=F32))
            if masked:
                k_pos = ks + lax.broadcasted_iota(jnp.int32, (tk, chunk), 0)
                q_pos = qi * tq + (lax.broadcasted_iota(jnp.int32, (tk, chunk), 1) & (tq - 1))
                s = jnp.where(k_pos <= q_pos, s, -jnp.inf)
            m_old = m_s[:, sl]
            m_new = jnp.maximum(m_old, jnp.max(s, axis=0, keepdims=True))
            alpha = jnp.exp(m_old - m_new)
            p = jnp.exp(s - m_new)
            l_s[:, sl] = alpha * l_s[:, sl] + jnp.sum(p, axis=0, keepdims=True)
            acc_s[:, sl] = alpha * acc_s[:, sl] + jnp.dot(kct, p.astype(BF16), preferred_element_type=F32)
            m_s[:, sl] = m_new

    n_full = (qi * tq) // tk

    def body(kb, carry):
        step(kb, False)
        return carry

    lax.fori_loop(0, n_full, body, 0)
    step(n_full, True)

    for h in range(H):
        sl = slice(h * tq, (h + 1) * tq)
        o_t = (acc_s[:, sl] / l_s[:, sl]).astype(BF16)
        ob = lax.dot_general(o_t, wuv_ref[h], TN_DIMS, preferred_element_type=F32)
        z = zb_ref[:, h * vdim:(h + 1) * vdim].astype(F32)
        o_ref[:, h * vdim:(h + 1) * vdim] = (ob * (z * _sigmoid(z))).astype(BF16)


def _attn_prompt(qn, qpe_h, ckv_b, ckv_t, kpe_b, wuk, wuv, u, *, rows_total, B, S, H, tq, tk, col, scale, name):
    nq = S // tq
    nope, rkv = wuk.shape[1], wuk.shape[2]
    vdim = wuv.shape[2]
    rp = qpe_h.shape[-1]
    hv = H * vdim
    chunk = _pick_tile(H * tq, (ATTN_CHUNK, 2 * tq, tq))
    kern = functools.partial(_attn_prompt_kernel, H=H, tq=tq, tk=tk, nope=nope, vdim=vdim, scale=scale, chunk=chunk)
    once = pl.Buffered(1)
    return pl.pallas_call(
        kern,
        grid=(B, nq),
        in_specs=[pl.BlockSpec((tq, H * nope), lambda b, q: (b * nq + q, 0)),
                  pl.BlockSpec((H, tq, rp), lambda b, q: (0, b * nq + q, 0)),
                  pl.BlockSpec((S, rkv), lambda b, q: (b, 0), pipeline_mode=once),
                  pl.BlockSpec((S // tk, rkv, tk), lambda b, q: (b, 0, 0), pipeline_mode=once),
                  pl.BlockSpec((S, rp), lambda b, q: (b, 0), pipeline_mode=once),
                  pl.BlockSpec(wuk.shape, lambda b, q: (0, 0, 0), pipeline_mode=once),
                  pl.BlockSpec(wuv.shape, lambda b, q: (0, 0, 0), pipeline_mode=once),
                  pl.BlockSpec((tq, hv), lambda b, q: (b * nq + q, col["zb"] // hv))],
        out_specs=pl.BlockSpec((tq, hv), lambda b, q: (b * nq + q, 0)),
        out_shape=jax.ShapeDtypeStruct((rows_total, hv), BF16),
        scratch_shapes=[pltpu.VMEM((H * tq, rkv), BF16), pltpu.VMEM((1, H * tq), F32),
                        pltpu.VMEM((1, H * tq), F32), pltpu.VMEM((rkv, H * tq), F32)],
        compiler_params=_params("parallel", "arbitrary"),
        name=name,
    )(qn, qpe_h, ckv_b, ckv_t, kpe_b, wuk, wuv, u)


def _qlat_kernel(qn_ref, wuk_ref, o_ref, *, scale):
    o_ref[0] = (jnp.dot(qn_ref[...], wuk_ref[0], preferred_element_type=F32) * scale).astype(BF16)


def _qlat(qn, wuk, *, row0, rows, scale, name):
    H, nope, rkv = wuk.shape
    assert row0 % rows == 0
    return pl.pallas_call(
        functools.partial(_qlat_kernel, scale=scale),
        grid=(H,),
        in_specs=[pl.BlockSpec((rows, nope), lambda h: (row0 // rows, h)),
                  pl.BlockSpec((1, nope, rkv), lambda h: (h, 0, 0))],
        out_specs=pl.BlockSpec((1, rows, rkv), lambda h: (h, 0, 0)),
        out_shape=jax.ShapeDtypeStruct((H, rows, rkv), BF16),
        compiler_params=_params("parallel"),
        name=name,
    )(qn, wuk)


def _ouv_kernel(ol_ref, wuv_ref, zb_ref, buf_ref, o_ref):
    del buf_ref
    ob = jnp.dot(ol_ref[0], wuv_ref[0], preferred_element_type=F32)
    z = zb_ref[...].astype(F32)
    o_ref[...] = (ob * (z * _sigmoid(z))).astype(BF16)


def _ouv(o_lat, wuv, u, ob_buf, *, row0, rows, zb_col, name):
    H, rkv, vdim = wuv.shape
    return pl.pallas_call(
        _ouv_kernel,
        grid=(H,),
        in_specs=[pl.BlockSpec((1, rows, rkv), lambda h: (h, 0, 0)),
                  pl.BlockSpec((1, rkv, vdim), lambda h: (h, 0, 0)),
                  pl.BlockSpec((rows, vdim), lambda h: (row0 // rows, zb_col // vdim + h)),
                  _ANY],
        out_specs=pl.BlockSpec((rows, vdim), lambda h: (row0 // rows, h)),
        out_shape=jax.ShapeDtypeStruct(ob_buf.shape, BF16),
        input_output_aliases={3: 0},
        compiler_params=_params("parallel"),
        name=name,
    )(o_lat, wuv, u, ob_buf)


def _attn_sample_kernel(pt_ref, ql_ref, qp_ref, cn_ref, kn_ref, *rest, P, T, n_steps):
    ckv_pages = rest[:P]
    kpe_pages = rest[P:2 * P]
    o_ref, m_s, l_s, acc_s = rest[2 * P:]
    j = pl.program_id(1)
    ql = ql_ref[0]
    qp = qp_ref[0]

    @pl.when(j == 0)
    def _():
        m_s[...] = jnp.full(m_s.shape, -jnp.inf, F32)
        l_s[...] = jnp.zeros_like(l_s)
        acc_s[...] = jnp.zeros_like(acc_s)

    def update(s, v):
        m_old = m_s[...]
        m_new = jnp.maximum(m_old, jnp.max(s, axis=-1, keepdims=True))
        alpha = jnp.exp(m_old - m_new)
        p = jnp.exp(s - m_new)
        l_s[...] = alpha * l_s[...] + jnp.sum(p, axis=-1, keepdims=True)
        acc_s[...] = alpha * acc_s[...] + jnp.dot(p.astype(BF16), v, preferred_element_type=F32)
        m_s[...] = m_new

    kc = jnp.concatenate([r[0, 0].astype(BF16) for r in ckv_pages], axis=0)
    kpt = jnp.concatenate([r[0, 0].astype(BF16) for r in kpe_pages], axis=1)
    update(lax.dot_general(ql, kc, NT_DIMS, preferred_element_type=F32)
           + jnp.dot(qp, kpt, preferred_element_type=F32), kc)

    @pl.when(j == n_steps - 1)
    def _():
        cn = cn_ref[0]
        kn = kn_ref[0]
        s = (lax.dot_general(ql, cn, NT_DIMS, preferred_element_type=F32)
             + lax.dot_general(qp, kn, NT_DIMS, preferred_element_type=F32))
        t_q = lax.broadcasted_iota(jnp.int32, s.shape, 0) & (T - 1)
        t_k = lax.broadcasted_iota(jnp.int32, s.shape, 1)
        update(jnp.where(t_k <= t_q, s, -jnp.inf), cn)
        o_ref[0] = (acc_s[...] / l_s[...]).astype(BF16)


def _attn_sample(page_table, ql, qp, cn, kn, cache_ckv, cache_kpe_t, *, layer, T, P, name):
    Bd, rows, rkv = ql.shape
    rope = qp.shape[-1]
    n_pages = page_table.shape[1]
    page = cache_ckv.shape[2]
    n_steps = n_pages // P
    assert T & (T - 1) == 0 and n_pages % P == 0

    def page_spec(shape, i):
        return pl.BlockSpec((1, 1) + shape, lambda b, j, pt: (layer, pt[b * n_pages + j * P + i], 0, 0))

    per_b = lambda shape: pl.BlockSpec((1,) + shape, lambda b, j, pt: (b, 0, 0))
    grid_spec = pltpu.PrefetchScalarGridSpec(
        num_scalar_prefetch=1,
        grid=(Bd, n_steps),
        in_specs=([per_b((rows, rkv)), per_b((rows, rope)), per_b((SAMPLE_PAD, rkv)), per_b((SAMPLE_PAD, rope))]
                  + [page_spec((page, rkv), i) for i in range(P)]
                  + [page_spec((rope, page), i) for i in range(P)]),
        out_specs=per_b((rows, rkv)),
        scratch_shapes=[pltpu.VMEM((rows, 1), F32), pltpu.VMEM((rows, 1), F32), pltpu.VMEM((rows, rkv), F32)],
    )
    kern = functools.partial(_attn_sample_kernel, P=P, T=T, n_steps=n_steps)
    return pl.pallas_call(
        kern,
        grid_spec=grid_spec,
        out_shape=jax.ShapeDtypeStruct((Bd, rows, rkv), BF16),
        compiler_params=_params("parallel", "arbitrary"),
        name=name,
    )(page_table.reshape(-1), ql, qp, cn, kn, *([cache_ckv] * P), *([cache_kpe_t] * P))


def _mix_kernel(ha_ref, ob_ref, wpa_ref, wpb_ref, ga_ref, gb_ref, o_ref):
    ya = jnp.dot(ha_ref[...], wpa_ref[...], preferred_element_type=F32)
    yb = jnp.dot(ob_ref[...], wpb_ref[...], preferred_element_type=F32)
    o_ref[...] = (_sigmoid(ga_ref[...].astype(F32)) * ya + _sigmoid(gb_ref[...].astype(F32)) * yb).astype(BF16)


def _mix(ha, ob, wpa, wpb, u, *, tm, tn, col, name):
    rows, ka = ha.shape
    kb = ob.shape[1]
    n = wpa.shape[1]
    return pl.pallas_call(
        _mix_kernel,
        grid=(rows // tm, n // tn),
        in_specs=[pl.BlockSpec((tm, ka), lambda i, j: (i, 0)),
                  pl.BlockSpec((tm, kb), lambda i, j: (i, 0)),
                  pl.BlockSpec((ka, tn), lambda i, j: (0, j)),
                  pl.BlockSpec((kb, tn), lambda i, j: (0, j)),
                  pl.BlockSpec((tm, tn), lambda i, j: (i, col["ga"] // tn + j)),
                  pl.BlockSpec((tm, tn), lambda i, j: (i, col["gb"] // tn + j))],
        out_specs=pl.BlockSpec((tm, tn), lambda i, j: (i, j)),
        out_shape=jax.ShapeDtypeStruct((rows, n), BF16),
        compiler_params=_params("parallel", "arbitrary"),
        name=name,
    )(ha, ob, wpa, wpb, u, u)


def _out_proj_kernel(mix_ref, w_ref, x_ref, o_ref):
    o_ref[...] = x_ref[...] + jnp.dot(mix_ref[...], w_ref[...], preferred_element_type=F32)


def _out_proj(mix, w, x, *, tm, tn, name):
    rows, k = mix.shape
    n = w.shape[1]
    return pl.pallas_call(
        _out_proj_kernel,
        grid=(rows // tm, n // tn),
        in_specs=[pl.BlockSpec((tm, k), lambda i, j: (i, 0)),
                  pl.BlockSpec((k, tn), lambda i, j: (0, j)),
                  pl.BlockSpec((tm, tn), lambda i, j: (i, j))],
        out_specs=pl.BlockSpec((tm, tn), lambda i, j: (i, j)),
        out_shape=jax.ShapeDtypeStruct((rows, n), F32),
        compiler_params=_params("parallel", "arbitrary"),
        name=name,
    )(mix, w, x)


def _ple_kernel(xf_ref, g_ref, wpg_ref, p_ref, wple_ref, x_ref, o_ref, h_ref):
    @pl.when(pl.program_id(1) == 0)
    def _():
        h_ref[...] = _rms(xf_ref[...], g_ref[...]).astype(BF16)

    gate = _sigmoid(jnp.dot(h_ref[...], wpg_ref[...], preferred_element_type=F32))
    emb = jnp.dot(p_ref[...].astype(BF16), wple_ref[...], preferred_element_type=F32)
    o_ref[...] = x_ref[...] + emb * gate


def _ple(x, g, wpg, p, wple, *, tm, tn, name):
    rows, d = x.shape
    n = wpg.shape[1]
    pd = p.shape[1]
    return pl.pallas_call(
        _ple_kernel,
        grid=(rows // tm, n // tn),
        in_specs=[pl.BlockSpec((tm, d), lambda i, j: (i, 0)),
                  pl.BlockSpec((1, d), lambda i, j: (0, 0)),
                  pl.BlockSpec((d, tn), lambda i, j: (0, j)),
                  pl.BlockSpec((tm, pd), lambda i, j: (i, 0)),
                  pl.BlockSpec((pd, tn), lambda i, j: (0, j)),
                  pl.BlockSpec((tm, tn), lambda i, j: (i, j))],
        out_specs=pl.BlockSpec((tm, tn), lambda i, j: (i, j)),
        out_shape=jax.ShapeDtypeStruct((rows, n), F32),
        scratch_shapes=[pltpu.VMEM((tm, d), BF16)],
        compiler_params=_params("parallel", "arbitrary"),
        name=name,
    )(x, g, wpg, p, wple, x)


def _final_norm_kernel(x_ref, g_ref, o_ref):
    o_ref[...] = _rms(x_ref[...], g_ref[...])


def _final_norm(x, g, *, row0, nrows, tm, name):
    d = x.shape[1]
    blk0 = row0 // tm
    return pl.pallas_call(
        _final_norm_kernel,
        grid=(nrows // tm,),
        in_specs=[pl.BlockSpec((tm, d), lambda i: (blk0 + i, 0)), pl.BlockSpec((1, d), lambda i: (0, 0))],
        out_specs=pl.BlockSpec((tm, d), lambda i: (i, 0)),
        out_shape=jax.ShapeDtypeStruct((nrows, d), F32),
        compiler_params=_params("parallel"),
        name=name,
    )(x, g)


def _rope_tables(pos, rope):
    half = rope // 2
    inv = ROPE_THETA ** (-jnp.arange(half, dtype=F32) / half)
    ang = pos[:, None] * inv[None, :]
    cos, sin, zero = jnp.cos(ang), jnp.sin(ang), jnp.zeros_like(ang)
    reps = LANES // rope
    cos_t = jnp.tile(jnp.concatenate([cos, cos], axis=1), (1, reps))
    sin_lo = jnp.tile(jnp.concatenate([-sin, zero], axis=1), (1, reps))
    sin_hi = jnp.tile(jnp.concatenate([zero, sin], axis=1), (1, reps))
    return cos_t, sin_lo, sin_hi


def kernel(x_prompt, x_sample, p_prompt, p_sample, cache_ckv, cache_kpe, state_C, state_n, state_m,
           page_table, g_in, w_in, b_i, b_f, g_q, w_uq, g_kv, w_uk, w_uv, g_head, w_pa, w_pb,
           w_out, w_ple, g_ple, w_pg, g_final):
    B, S, D = x_prompt.shape
    Bd, T, _ = x_sample.shape
    depth = g_in.shape[0]
    H = b_i.shape[1]
    dv = D // H
    dk = dv // 2
    hq, hv = H * dk, H * dv
    rq, rkv = g_q.shape[1], g_kv.shape[1]
    HB = w_uk.shape[2]
    nope, vdim = w_uk.shape[3], w_uv.shape[3]
    rope = w_uq.shape[3] - nope
    wb = HB * vdim
    page = cache_ckv.shape[2]
    past_len = page_table.shape[1] * page
    mla_scale = float((nope + rope) ** -0.5)
    assert T <= SAMPLE_PAD and LANES % rope == 0 and rope + 2 * H <= LANES
    assert hv == D and wb == D and rq == rkv and (7 * D) % rq == 0

    n_p, n_s = B * S, Bd * SAMPLE_PAD
    rows = n_p + n_s
    tm = _pick_tile(math.gcd(n_p, n_s), (512, 256, 128, 64, 32, 16, 8))
    tm_in = _pick_tile(math.gcd(n_p, n_s), (INPROJ_TM, 512, 256, 128, 64, 32, 16, 8))

    pad_t = ((0, 0), (0, SAMPLE_PAD - T), (0, 0))
    x = jnp.concatenate([x_prompt.reshape(n_p, D), jnp.pad(x_sample, pad_t).reshape(n_s, D)], axis=0)
    pos = jnp.concatenate([jnp.tile(jnp.arange(S, dtype=F32), B),
                           jnp.tile(past_len + jnp.arange(SAMPLE_PAD, dtype=F32), Bd)])
    tables = _rope_tables(pos, rope)
    cache_kpe_t = jnp.swapaxes(cache_kpe, 2, 3)

    col = {"v": 0, "o": D, "za": 2 * D, "zb": 3 * D, "ga": 4 * D, "gb": 5 * D,
           "q": 6 * D, "k": 6 * D + hq, "cq": 7 * D, "ckv": 7 * D + rq}
    n_main = 7 * D + rq + rkv
    sizes = (hq, hq, hv, H, H, hv, hv, rq, rkv, rope, wb, D, D)
    starts = np.concatenate([[0], np.cumsum(sizes)])
    seg = {name: (int(starts[i]), int(starts[i + 1])) for i, name in enumerate(
        ("q", "k", "v", "i", "f", "o", "za", "cq", "ckv", "kr", "zb", "ga", "gb"))}
    main_order = ("v", "o", "za", "zb", "ga", "gb", "q", "k", "cq", "ckv")
    gate_lane = rope
    tn_main = _pick_tile(n_main, (1024, 512, 256, 128))
    tn_d = _pick_tile(D, (1024, 512, 256, 128))

    xs = x
    c_sample = None
    outs = {k: [] for k in ("ckv_p", "kpe_p", "C_p", "n_p", "m_p", "ckv_s", "kpe_s", "n_s", "m_s")}
    for l in range(depth):
        wl = w_in[l]
        w_main = jnp.concatenate([wl[:, seg[n][0]:seg[n][1]] for n in main_order], axis=1).astype(BF16)
        w_small = jnp.concatenate(
            [wl[:, seg[n][0]:seg[n][1]] for n in ("kr", "i", "f")]
            + [jnp.zeros((D, LANES - rope - 2 * H), F32)], axis=1).astype(BF16)
        gate_bias = jnp.concatenate([jnp.zeros((rope,), F32), b_i[l], b_f[l],
                                     jnp.zeros((LANES - rope - 2 * H,), F32)]).reshape(1, LANES)
        wq = w_uq[l]
        w_q = jnp.concatenate([wq[:, :, :nope].reshape(rq, HB * nope),
                               wq[:, :, nope:].reshape(rq, HB * rope)], axis=1).astype(BF16)
        wuk = jnp.transpose(w_uk[l], (1, 2, 0)).astype(BF16)
        wuv = jnp.transpose(w_uv[l], (1, 0, 2)).astype(BF16)

        g_in_l = g_in[l].reshape(1, D)
        u = _norm_matmul(xs, g_in_l, w_main, tm=tm_in, tn=tn_main, out_dtype=BF16, name=f"inproj_main_{l}")
        us = _norm_matmul(xs, g_in_l, w_small, tm=tm_in, tn=LANES, out_dtype=F32, name=f"inproj_small_{l}")

        g_head_l = g_head[l].reshape(1, hv)
        Lp = _pick_tile(S, (MLSTM_CHUNK, 128, 64, 32, 16, 8))
        common = dict(rows_total=rows, H=H, dk=dk, dv=dv, col=col, gate_lane=gate_lane)
        ha, C_p, nn_p, m_p = _mlstm(u, us, gate_bias, g_head_l, None, None, None, layer=0, c_layers=1,
                                    row0=0, B=B, S=S, L=Lp, t_valid=Lp, name=f"mlstm_prompt_{l}", **common)
        ha, c_sample, nn_s, m_s = _mlstm(u, us, gate_bias, g_head_l, (state_C, state_n, state_m), ha, c_sample,
                                         layer=l, c_layers=depth, row0=n_p, B=Bd, S=SAMPLE_PAD, L=SAMPLE_PAD,
                                         t_valid=T, name=f"mlstm_sample_{l}", **common)

        qn, qpe, ckv, ckv_b, kpe, kpe_b = _mla_prep(
            u, us, tables, g_q[l].reshape(1, rq), g_kv[l].reshape(1, rkv), w_q, tm=tm, rq=rq, rkv=rkv,
            col=col, n_nope=HB * nope, n_pe=HB * rope, rope=rope, scale=mla_scale, name=f"mla_prep_{l}")
        kpe_b = kpe_b[:, :rope]
        qpe_h = jnp.transpose(qpe.reshape(rows, HB, rope), (1, 0, 2))
        tq = _pick_tile(S, (ATTN_TQ, 64, 32, 16, 8))
        tk = _pick_tile(S, (ATTN_TK, 256, 128))
        ckv_t = jnp.transpose(ckv_b[:n_p].reshape(n_p // tk, tk, rkv), (0, 2, 1))
        ob = _attn_prompt(qn, qpe_h, ckv_b, ckv_t, kpe_b, wuk, wuv, u, rows_total=rows, B=B, S=S, H=HB,
                          tq=tq, tk=tk, col=col, scale=mla_scale, name=f"attn_prompt_{l}")

        ql_s = _qlat(qn, wuk, row0=n_p, rows=n_s, scale=mla_scale, name=f"qlat_sample_{l}")
        to_bt = lambda a: jnp.transpose(a.reshape(HB, Bd, SAMPLE_PAD, -1)[:, :, :T], (1, 0, 2, 3)).reshape(
            Bd, HB * T, a.shape[-1])
        ol_s = _attn_sample(page_table, to_bt(ql_s), to_bt(qpe_h[:, n_p:]),
                            ckv_b[n_p:].reshape(Bd, SAMPLE_PAD, rkv), kpe_b[n_p:].reshape(Bd, SAMPLE_PAD, rope),
                            cache_ckv, cache_kpe_t, layer=l, T=T,
                            P=_pick_tile(page_table.shape[1], (PAGES_PER_STEP, 16, 8, 4, 2, 1)),
                            name=f"attn_sample_{l}")
        ol_h = jnp.pad(jnp.transpose(ol_s.reshape(Bd, HB, T, rkv), (1, 0, 2, 3)),
                       ((0, 0), (0, 0), (0, SAMPLE_PAD - T), (0, 0))).reshape(HB, n_s, rkv)
        ob = _ouv(ol_h, wuv, u, ob, row0=n_p, rows=n_s, zb_col=col["zb"], name=f"ouv_sample_{l}")

        mix = _mix(ha, ob, w_pa[l].astype(BF16), w_pb[l].astype(BF16), u, tm=tm, tn=tn_d, col=col, name=f"mix_{l}")
        x1 = _out_proj(mix, w_out[l].astype(BF16), xs, tm=tm, tn=tn_d, name=f"out_proj_{l}")
        p_all = jnp.concatenate([p_prompt[l].reshape(n_p, -1),
                                 jnp.pad(p_sample[l], pad_t).reshape(n_s, -1)], axis=0)
        xs = _ple(x1, g_ple[l].reshape(1, D), w_pg[l].astype(BF16), p_all, w_ple[l].astype(BF16),
                  tm=tm, tn=tn_d, name=f"ple_{l}")

        unpad = lambda a: a[n_p:].reshape(Bd, SAMPLE_PAD, -1)[:, :T]
        outs["ckv_p"].append(ckv[:n_p].reshape(B, S, rkv))
        outs["kpe_p"].append(kpe[:n_p, :rope].reshape(B, S, rope))
        outs["C_p"].append(C_p[0]); outs["n_p"].append(nn_p); outs["m_p"].append(m_p)
        outs["ckv_s"].append(unpad(ckv))
        outs["kpe_s"].append(unpad(kpe)[:, :, :rope])
        outs["n_s"].append(nn_s); outs["m_s"].append(m_s)

    g_fin = g_final.reshape(1, D)
    y_prompt = _final_norm(xs, g_fin, row0=0, nrows=n_p, tm=tm, name="final_norm_prompt").reshape(B, S, D)
    y_sample = _final_norm(xs, g_fin, row0=n_p, nrows=n_s, tm=tm, name="final_norm_sample").reshape(
        Bd, SAMPLE_PAD, D)[:, :T]
    st = lambda k: jnp.stack(outs[k])
    return (y_prompt, y_sample, st("ckv_p"), st("kpe_p"), st("C_p"), st("n_p"), st("m_p"),
            st("ckv_s"), st("kpe_s"), c_sample, st("n_s"), st("m_s"))
```

```python
import functools
import math

import jax
import jax.numpy as jnp
import numpy as np
from jax import lax
from jax.experimental import pallas as pl
from jax.experimental.pallas import tpu as pltpu

EPS = 1e-6
ROPE_THETA = 10000.0
F32 = jnp.float32
BF16 = jnp.bfloat16

LANES = 128
SUBLANES = 8
MXU_COLS = 256
SAMPLE_PAD = SUBLANES
VMEM_LIMIT = 56 * 1024 * 1024
MLSTM_CHUNK = 256
MLSTM_SEQS_PER_STEP = 4
ATTN_TQ = 128
ATTN_TK = 512
ATTN_CHUNK = 2 * MXU_COLS
PAGES_PER_STEP = 32
INPROJ_TM = 1024

NT_DIMS = (((1,), (1,)), ((), ()))
TN_DIMS = (((0,), (0,)), ((), ()))


def _params(*sem):
    return pltpu.CompilerParams(dimension_semantics=sem, vmem_limit_bytes=VMEM_LIMIT)


def _pick_tile(n, prefs):
    for t in prefs:
        if n % t == 0:
            return t
    return n


def _sigmoid(x):
    return 1.0 / (1.0 + jnp.exp(-x))


def _log_sigmoid(x):
    return jnp.minimum(x, 0.0) - jnp.log(1.0 + jnp.exp(-jnp.abs(x)))


def _rms(x, g):
    return x * lax.rsqrt(jnp.mean(x * x, axis=-1, keepdims=True) + EPS) * g


_ANY = pl.BlockSpec(memory_space=pl.ANY)


def _norm_matmul_kernel(x_ref, g_ref, w_ref, o_ref, h_ref):
    @pl.when(pl.program_id(1) == 0)
    def _():
        h_ref[...] = _rms(x_ref[...], g_ref[...]).astype(BF16)

    o_ref[...] = jnp.dot(h_ref[...], w_ref[...], preferred_element_type=F32).astype(o_ref.dtype)


def _norm_matmul(x, g, w, *, tm, tn, out_dtype, name):
    m, k = x.shape
    n = w.shape[1]
    return pl.pallas_call(
        _norm_matmul_kernel,
        grid=(m // tm, n // tn),
        in_specs=[pl.BlockSpec((tm, k), lambda i, j: (i, 0)),
                  pl.BlockSpec((1, k), lambda i, j: (0, 0)),
                  pl.BlockSpec((k, tn), lambda i, j: (0, j))],
        out_specs=pl.BlockSpec((tm, tn), lambda i, j: (i, j)),
        out_shape=jax.ShapeDtypeStruct((m, n), out_dtype),
        scratch_shapes=[pltpu.VMEM((tm, k), BF16)],
        compiler_params=_params("parallel", "arbitrary"),
        name=name,
    )(x, g, w)


def _mlstm_kernel(*refs, H, dk, dv, L, BB, t_valid, nc, has_init, n_alias, gate_lane):
    q_ref, k_ref, v_ref, o_ref, z_ref, s_ref, bias_ref, gh_ref = refs[:8]
    pos = 8
    if has_init:
        c0_ref, n0_ref, m0_ref = refs[pos:pos + 3]
        pos += 3
    pos += n_alias
    h_out, c_out, n_out, m_out, c_s, n_s, m_s = refs[pos:]
    c = pl.program_id(1)

    @pl.when(c == 0)
    def _():
        if has_init:
            c_s[...] = c0_ref[0]
            n_s[...] = n0_ref[0]
            m_s[...] = m0_ref[0]
        else:
            c_s[...] = jnp.zeros_like(c_s)
            n_s[...] = jnp.zeros_like(n_s)
            m_s[...] = jnp.zeros_like(m_s)

    row = lax.broadcasted_iota(jnp.int32, (L, L), 0)
    col = lax.broadcasted_iota(jnp.int32, (L, L), 1)
    causal = col <= row
    eye = col == row
    tok = lax.broadcasted_iota(jnp.int32, (L, 1), 0)
    head_row = lax.broadcasted_iota(jnp.int32, (H, dk), 0)
    head_lane = lax.broadcasted_iota(jnp.int32, (1, H), 1)
    scale = dk ** -0.5

    for bb, h in [(bb, h) for bb in range(BB) for h in range(H)]:
        r = slice(bb * L, (bb + 1) * L)
        if h == 0:
            n_all, m_all = n_s[bb], m_s[bb]
            n_acc, m_acc = n_all, m_all
            gates = s_ref[r, :] + bias_ref[...]
            csum = jnp.dot(causal.astype(F32), _log_sigmoid(gates), preferred_element_type=F32,
                           precision=lax.Precision.HIGHEST)
        qb = q_ref[r, h * dk:(h + 1) * dk]
        kb = k_ref[r, h * dk:(h + 1) * dk]
        vb = v_ref[r, h * dv:(h + 1) * dv]
        icol = gates[:, gate_lane + h:gate_lane + h + 1]
        bcol = csum[:, gate_lane + H + h:gate_lane + H + h + 1]
        wcol = icol - bcol
        wrow = jnp.sum(jnp.where(eye, wcol, 0.0), axis=0, keepdims=True)
        d = jnp.where(causal, bcol + wrow, -jnp.inf)
        m_prev = m_all[:, h:h + 1]
        inter = bcol + m_prev
        m_t = jnp.maximum(inter, jnp.max(d, axis=-1, keepdims=True))
        s_inter = jnp.exp(inter - m_t) * scale
        a = lax.dot_general(qb, kb, NT_DIMS, preferred_element_type=F32) * (jnp.exp(d - m_t) * scale)
        c_old = c_s[bb, h]
        num = (s_inter * lax.dot_general(qb, c_old.astype(BF16), NT_DIMS, preferred_element_type=F32)
               + jnp.dot(a.astype(BF16), vb, preferred_element_type=F32))
        n_old = n_all[h:h + 1, :]
        den = (s_inter * jnp.sum(qb.astype(F32) * n_old, axis=-1, keepdims=True)
               + jnp.sum(a, axis=-1, keepdims=True))
        hval = num / jnp.maximum(jnp.abs(den), jnp.exp(-m_t))

        m_new = m_t[t_valid - 1:t_valid, :]
        b_last = bcol[t_valid - 1:t_valid, :]
        g_state = jnp.exp(b_last + m_prev - m_new)
        g_tok = jnp.exp(wcol + (b_last - m_new))
        if t_valid < L:
            g_tok = jnp.where(tok < t_valid, g_tok, 0.0)
        vg = (vb.astype(F32) * g_tok).astype(BF16)
        c_s[bb, h] = g_state * c_old + lax.dot_general(vg, kb, TN_DIMS, preferred_element_type=F32)
        n_new = g_state * n_old + jnp.sum(kb.astype(F32) * g_tok, axis=0, keepdims=True)
        n_acc = jnp.where(head_row == h, n_new, n_acc)
        m_acc = jnp.where(head_lane == h, m_new, m_acc)
        if h == H - 1:
            n_s[bb] = n_acc
            m_s[bb] = m_acc

        hg = hval * _sigmoid(o_ref[r, h * dv:(h + 1) * dv].astype(F32))
        hn = _rms(hg, gh_ref[:, h * dv:(h + 1) * dv])
        z = z_ref[r, h * dv:(h + 1) * dv].astype(F32)
        h_out[r, h * dv:(h + 1) * dv] = (hn * (z * _sigmoid(z))).astype(BF16)

    @pl.when(c == nc - 1)
    def _():
        c_out[0] = c_s[...]
        n_out[...] = n_s[...]
        m_out[...] = m_s[...]


def _mlstm(u, us, gate_bias, g_head, init, h_buf, c_buf, *, layer, c_layers, rows_total, row0, B, S, L, t_valid,
           H, dk, dv, col, gate_lane, name):
    nc = S // L
    BB = _pick_tile(B, (MLSTM_SEQS_PER_STEP, 2, 1)) if nc == 1 else 1
    rb = BB * L
    assert row0 % rb == 0
    blk0 = row0 // rb
    hq, hv = H * dk, H * dv

    def rows(width, off):
        return pl.BlockSpec((rb, width), lambda b, c: (blk0 + b * nc + c, off // width))

    in_specs = [rows(hq, col["q"]), rows(hq, col["k"]), rows(hv, col["v"]), rows(hv, col["o"]),
                rows(hv, col["za"]), rows(LANES, 0),
                pl.BlockSpec((1, LANES), lambda b, c: (0, 0)),
                pl.BlockSpec((1, hv), lambda b, c: (0, 0))]
    args = [u, u, u, u, u, us, gate_bias, g_head]
    if init is not None:
        c0, n0, m0 = init
        in_specs += [pl.BlockSpec((1, BB, H, dv, dk), lambda b, c: (layer, b, 0, 0, 0)),
                     pl.BlockSpec((1, BB, H, dk), lambda b, c: (layer, b, 0, 0)),
                     pl.BlockSpec((1, BB, 1, H), lambda b, c: (layer, b, 0, 0))]
        args += [c0, n0, m0.reshape(m0.shape[0], B, 1, H)]
    aliases = {}
    for buf, out_idx in ((h_buf, 0), (c_buf, 1)):
        if buf is not None:
            aliases[len(args)] = out_idx
            in_specs.append(_ANY)
            args.append(buf)
    kern = functools.partial(_mlstm_kernel, H=H, dk=dk, dv=dv, L=L, BB=BB, t_valid=t_valid, nc=nc,
                             has_init=init is not None, n_alias=len(aliases), gate_lane=gate_lane)
    h_out, c_out, n_out, m_out = pl.pallas_call(
        kern,
        grid=(B // BB, nc),
        in_specs=in_specs,
        out_specs=[pl.BlockSpec((rb, hv), lambda b, c: (blk0 + b * nc + c, 0)),
                   pl.BlockSpec((1, BB, H, dv, dk), lambda b, c: (layer, b, 0, 0, 0)),
                   pl.BlockSpec((BB, H, dk), lambda b, c: (b, 0, 0)),
                   pl.BlockSpec((BB, 1, H), lambda b, c: (b, 0, 0))],
        out_shape=[jax.ShapeDtypeStruct((rows_total, hv), BF16),
                   jax.ShapeDtypeStruct((c_layers, B, H, dv, dk), F32),
                   jax.ShapeDtypeStruct((B, H, dk), F32),
                   jax.ShapeDtypeStruct((B, 1, H), F32)],
        scratch_shapes=[pltpu.VMEM((BB, H, dv, dk), F32), pltpu.VMEM((BB, H, dk), F32),
                        pltpu.VMEM((BB, 1, H), F32)],
        input_output_aliases=aliases,
        compiler_params=_params("parallel", "arbitrary"),
        name=name,
    )(*args)
    return h_out, c_out, n_out, m_out.reshape(B, H)


def _rope_lanes(x, cos_t, sin_lo, sin_hi):
    return x * cos_t + pltpu.roll(x, 96, 1) * sin_lo + pltpu.roll(x, 32, 1) * sin_hi


def _mla_prep_kernel(cq_ref, ckv_ref, s_ref, cos_ref, slo_ref, shi_ref, gq_ref, gkv_ref, wq_ref,
                     qn_out, qpe_out, ckv_out, ckvb_out, kpe_out, kpeb_out, *, n_nope, n_pe, rope, scale):
    cos_t, sin_lo, sin_hi = cos_ref[...], slo_ref[...], shi_ref[...]
    cqn = _rms(cq_ref[...].astype(F32), gq_ref[...]).astype(BF16)
    qf = jnp.dot(cqn, wq_ref[...], preferred_element_type=F32)
    qn_out[...] = qf[:, :n_nope].astype(BF16)
    for c in range(n_pe // LANES):
        blk = qf[:, n_nope + c * LANES:n_nope + (c + 1) * LANES]
        qpe_out[:, c * LANES:(c + 1) * LANES] = (_rope_lanes(blk, cos_t, sin_lo, sin_hi) * scale).astype(BF16)
    ckv = _rms(ckv_ref[...].astype(F32), gkv_ref[...])
    ckv_out[...] = ckv
    ckvb_out[...] = ckv.astype(BF16)
    lane = lax.broadcasted_iota(jnp.int32, s_ref.shape, 1)
    kpe = jnp.where(lane < rope, _rope_lanes(s_ref[...], cos_t, sin_lo, sin_hi), 0.0)
    kpe_out[...] = kpe
    kpeb_out[...] = kpe.astype(BF16)


def _mla_prep(u, us, tables, g_q, g_kv, w_q, *, tm, rq, rkv, col, n_nope, n_pe, rope, scale, name):
    rows = u.shape[0]
    cos_t, sin_lo, sin_hi = tables
    tok = lambda width, off: pl.BlockSpec((tm, width), lambda i: (i, off // width))
    const = lambda a: pl.BlockSpec(a.shape, lambda i: (0,) * a.ndim)
    kern = functools.partial(_mla_prep_kernel, n_nope=n_nope, n_pe=n_pe, rope=rope, scale=scale)
    return pl.pallas_call(
        kern,
        grid=(rows // tm,),
        in_specs=[tok(rq, col["cq"]), tok(rkv, col["ckv"]), tok(LANES, 0), tok(LANES, 0), tok(LANES, 0),
                  tok(LANES, 0), const(g_q), const(g_kv), const(w_q)],
        out_specs=[tok(n_nope, 0), tok(n_pe, 0), tok(rkv, 0), tok(rkv, 0), tok(LANES, 0), tok(LANES, 0)],
        out_shape=[jax.ShapeDtypeStruct((rows, n_nope), BF16), jax.ShapeDtypeStruct((rows, n_pe), BF16),
                   jax.ShapeDtypeStruct((rows, rkv), F32), jax.ShapeDtypeStruct((rows, rkv), BF16),
                   jax.ShapeDtypeStruct((rows, LANES), F32), jax.ShapeDtypeStruct((rows, LANES), BF16)],
        compiler_params=_params("parallel"),
        name=name,
    )(u, u, us, cos_t, sin_lo, sin_hi, g_q, g_kv, w_q)


def _attn_prompt_kernel(qn_ref, qpe_ref, ckv_ref, ckvt_ref, kpe_ref, wuk_ref, wuv_ref, zb_ref, o_ref,
                        qlat_s, m_s, l_s, acc_s, *, H, tq, tk, nope, vdim, scale, chunk):
    qi = pl.program_id(1)
    cols = H * tq
    for h in range(H):
        ql = jnp.dot(qn_ref[:, h * nope:(h + 1) * nope], wuk_ref[h], preferred_element_type=F32)
        qlat_s[h * tq:(h + 1) * tq, :] = (ql * scale).astype(BF16)
    m_s[...] = jnp.full(m_s.shape, -jnp.inf, F32)
    l_s[...] = jnp.zeros_like(l_s)
    acc_s[...] = jnp.zeros_like(acc_s)

    def step(kb, masked):
        ks = pl.multiple_of(kb * tk, tk)
        kc = ckv_ref[pl.ds(ks, tk), :]
        kp = kpe_ref[pl.ds(ks, tk), :]
        kct = ckvt_ref[kb]
        def scores(c0):
            qp = qpe_ref[c0 // tq:(c0 + chunk) // tq].reshape(chunk, qpe_ref.shape[-1])
            return (lax.dot_general(kc, qlat_s[c0:c0 + chunk, :], NT_DIMS, preferred_element_type=F32)
                    + lax.dot_general(kp, qp, NT_DIMS, preferred_element_type=F32))

        s_next = scores(0)
        for c0 in range(0, cols, chunk):
            sl = slice(c0, c0 + chunk)
            s = s_next
            if c0 + chunk < cols:
                s_next = scores(c0 + chunk)
            if masked:
                k_pos = ks + lax.broadcasted_iota(jnp.int32, (tk, chunk), 0)
                q_pos = qi * tq + (lax.broadcasted_iota(jnp.int32, (tk, chunk), 1) & (tq - 1))
                s = jnp.where(k_pos <= q_pos, s, -jnp.inf)
            m_old = m_s[:, sl]
            m_new = jnp.maximum(m_old, jnp.max(s, axis=0, keepdims=True))
            alpha = jnp.exp(m_old - m_new)
            p = jnp.exp(s - m_new)
            l_s[:, sl] = alpha * l_s[:, sl] + jnp.sum(p, axis=0, keepdims=True)
            acc_s[:, sl] = alpha * acc_s[:, sl] + jnp.dot(kct, p.astype(BF16), preferred_element_type=F32)
            m_s[:, sl] = m_new

    n_full = (qi * tq) // tk

    def body(kb, carry):
        step(kb, False)
        return carry

    lax.fori_loop(0, n_full, body, 0)
    step(n_full, True)

    for h in range(H):
        sl = slice(h * tq, (h + 1) * tq)
        o_t = (acc_s[:, sl] / l_s[:, sl]).astype(BF16)
        ob = lax.dot_general(o_t, wuv_ref[h], TN_DIMS, preferred_element_type=F32)
        z = zb_ref[:, h * vdim:(h + 1) * vdim].astype(F32)
        o_ref[:, h * vdim:(h + 1) * vdim] = (ob * (z * _sigmoid(z))).astype(BF16)


def _attn_prompt(qn, qpe_h, ckv_b, ckv_t, kpe_b, wuk, wuv, u, *, rows_total, B, S, H, tq, tk, col, scale, name):
    nq = S // tq
    nope, rkv = wuk.shape[1], wuk.shape[2]
    vdim = wuv.shape[2]
    rp = qpe_h.shape[-1]
    hv = H * vdim
    chunk = _pick_tile(H * tq, (ATTN_CHUNK, 2 * tq, tq))
    kern = functools.partial(_attn_prompt_kernel, H=H, tq=tq, tk=tk, nope=nope, vdim=vdim, scale=scale, chunk=chunk)
    once = pl.Buffered(1)
    return pl.pallas_call(
        kern,
        grid=(B, nq),
        in_specs=[pl.BlockSpec((tq, H * nope), lambda b, q: (b * nq + q, 0)),
                  pl.BlockSpec((H, tq, rp), lambda b, q: (0, b * nq + q, 0)),
                  pl.BlockSpec((S, rkv), lambda b, q: (b, 0), pipeline_mode=once),
                  pl.BlockSpec((S // tk, rkv, tk), lambda b, q: (b, 0, 0), pipeline_mode=once),
                  pl.BlockSpec((S, rp), lambda b, q: (b, 0), pipeline_mode=once),
                  pl.BlockSpec(wuk.shape, lambda b, q: (0, 0, 0), pipeline_mode=once),
                  pl.BlockSpec(wuv.shape, lambda b, q: (0, 0, 0), pipeline_mode=once),
                  pl.BlockSpec((tq, hv), lambda b, q: (b * nq + q, col["zb"] // hv))],
        out_specs=pl.BlockSpec((tq, hv), lambda b, q: (b * nq + q, 0)),
        out_shape=jax.ShapeDtypeStruct((rows_total, hv), BF16),
        scratch_shapes=[pltpu.VMEM((H * tq, rkv), BF16), pltpu.VMEM((1, H * tq), F32),
                        pltpu.VMEM((1, H * tq), F32), pltpu.VMEM((rkv, H * tq), F32)],
        compiler_params=_params("parallel", "arbitrary"),
        name=name,
    )(qn, qpe_h, ckv_b, ckv_t, kpe_b, wuk, wuv, u)


def _qlat_kernel(qn_ref, wuk_ref, o_ref, *, scale):
    o_ref[0] = (jnp.dot(qn_ref[...], wuk_ref[0], preferred_element_type=F32) * scale).astype(BF16)


def _qlat(qn, wuk, *, row0, rows, scale, name):
    H, nope, rkv = wuk.shape
    assert row0 % rows == 0
    return pl.pallas_call(
        functools.partial(_qlat_kernel, scale=scale),
        grid=(H,),
        in_specs=[pl.BlockSpec((rows, nope), lambda h: (row0 // rows, h)),
                  pl.BlockSpec((1, nope, rkv), lambda h: (h, 0, 0))],
        out_specs=pl.BlockSpec((1, rows, rkv), lambda h: (h, 0, 0)),
        out_shape=jax.ShapeDtypeStruct((H, rows, rkv), BF16),
        compiler_params=_params("parallel"),
        name=name,
    )(qn, wuk)


def _ouv_kernel(ol_ref, wuv_ref, zb_ref, buf_ref, o_ref):
    del buf_ref
    ob = jnp.dot(ol_ref[0], wuv_ref[0], preferred_element_type=F32)
    z = zb_ref[...].astype(F32)
    o_ref[...] = (ob * (z * _sigmoid(z))).astype(BF16)


def _ouv(o_lat, wuv, u, ob_buf, *, row0, rows, zb_col, name):
    H, rkv, vdim = wuv.shape
    return pl.pallas_call(
        _ouv_kernel,
        grid=(H,),
        in_specs=[pl.BlockSpec((1, rows, rkv), lambda h: (h, 0, 0)),
                  pl.BlockSpec((1, rkv, vdim), lambda h: (h, 0, 0)),
                  pl.BlockSpec((rows, vdim), lambda h: (row0 // rows, zb_col // vdim + h)),
                  _ANY],
        out_specs=pl.BlockSpec((rows, vdim), lambda h: (row0 // rows, h)),
        out_shape=jax.ShapeDtypeStruct(ob_buf.shape, BF16),
        input_output_aliases={3: 0},
        compiler_params=_params("parallel"),
        name=name,
    )(o_lat, wuv, u, ob_buf)


def _attn_sample_kernel(pt_ref, ql_ref, qp_ref, cn_ref, kn_ref, ckv_hbm, kpe_hbm, o_ref,
                        ckv_buf, kpe_buf, sems, m_s, l_s, acc_s, *, layer, P, T, n_batch, n_steps, n_pages):
    page = ckv_buf.shape[1] // P
    b = pl.program_id(0)
    j = pl.program_id(1)
    step = b * n_steps + j
    slot = step % 2

    def page_copies(bb, jj, sl):
        out = []
        for i in range(P):
            pid = pt_ref[bb * n_pages + jj * P + i]
            out.append(pltpu.make_async_copy(ckv_hbm.at[layer, pid], ckv_buf.at[sl, pl.ds(i * page, page)],
                                             sems.at[sl, 0]))
            out.append(pltpu.make_async_copy(kpe_hbm.at[layer, pid], kpe_buf.at[sl, i], sems.at[sl, 1]))
        return out

    @pl.when(step == 0)
    def _():
        for cp in page_copies(b, j, slot):
            cp.start()

    @pl.when(step + 1 < n_batch * n_steps)
    def _():
        wrap = j == n_steps - 1
        for cp in page_copies(jnp.where(wrap, b + 1, b), jnp.where(wrap, 0, j + 1), 1 - slot):
            cp.start()

    for cp in page_copies(b, j, slot):
        cp.wait()

    ql = ql_ref[0]
    qp = qp_ref[0]

    @pl.when(j == 0)
    def _():
        m_s[...] = jnp.full(m_s.shape, -jnp.inf, F32)
        l_s[...] = jnp.zeros_like(l_s)
        acc_s[...] = jnp.zeros_like(acc_s)

    def update(s, v):
        m_old = m_s[...]
        m_new = jnp.maximum(m_old, jnp.max(s, axis=-1, keepdims=True))
        alpha = jnp.exp(m_old - m_new)
        p = jnp.exp(s - m_new)
        l_s[...] = alpha * l_s[...] + jnp.sum(p, axis=-1, keepdims=True)
        acc_s[...] = alpha * acc_s[...] + jnp.dot(p.astype(BF16), v, preferred_element_type=F32)
        m_s[...] = m_new

    kc = ckv_buf[slot].astype(BF16)
    kpt = jnp.concatenate([kpe_buf[slot, i].astype(BF16) for i in range(P)], axis=1)
    update(lax.dot_general(ql, kc, NT_DIMS, preferred_element_type=F32)
           + jnp.dot(qp, kpt, preferred_element_type=F32), kc)

    @pl.when(j == n_steps - 1)
    def _():
        cn = cn_ref[0]
        kn = kn_ref[0]
        s = (lax.dot_general(ql, cn, NT_DIMS, preferred_element_type=F32)
             + lax.dot_general(qp, kn, NT_DIMS, preferred_element_type=F32))
        t_q = lax.broadcasted_iota(jnp.int32, s.shape, 0) & (T - 1)
        t_k = lax.broadcasted_iota(jnp.int32, s.shape, 1)
        update(jnp.where(t_k <= t_q, s, -jnp.inf), cn)
        o_ref[0] = (acc_s[...] / l_s[...]).astype(BF16)


def _attn_sample(page_table, ql, qp, cn, kn, cache_ckv, cache_kpe_t, *, layer, T, P, name):
    Bd, rows, rkv = ql.shape
    rope = qp.shape[-1]
    n_pages = page_table.shape[1]
    page = cache_ckv.shape[2]
    n_steps = n_pages // P
    assert T & (T - 1) == 0 and n_pages % P == 0

    per_b = lambda shape: pl.BlockSpec((1,) + shape, lambda b, j, pt: (b, 0, 0))
    grid_spec = pltpu.PrefetchScalarGridSpec(
        num_scalar_prefetch=1,
        grid=(Bd, n_steps),
        in_specs=[per_b((rows, rkv)), per_b((rows, rope)), per_b((SAMPLE_PAD, rkv)), per_b((SAMPLE_PAD, rope)),
                  _ANY, _ANY],
        out_specs=per_b((rows, rkv)),
        scratch_shapes=[pltpu.VMEM((2, P * page, rkv), F32),
                        pltpu.VMEM((2, P, rope, page), F32),
                        pltpu.SemaphoreType.DMA((2, 2)),
                        pltpu.VMEM((rows, 1), F32), pltpu.VMEM((rows, 1), F32), pltpu.VMEM((rows, rkv), F32)],
    )
    kern = functools.partial(_attn_sample_kernel, layer=layer, P=P, T=T, n_batch=Bd, n_steps=n_steps,
                             n_pages=n_pages)
    return pl.pallas_call(
        kern,
        grid_spec=grid_spec,
        out_shape=jax.ShapeDtypeStruct((Bd, rows, rkv), BF16),
        compiler_params=_params("arbitrary", "arbitrary"),
        name=name,
    )(page_table.reshape(-1), ql, qp, cn, kn, cache_ckv, cache_kpe_t)


def _mix_kernel(ha_ref, ob_ref, wpa_ref, wpb_ref, ga_ref, gb_ref, o_ref):
    ya = jnp.dot(ha_ref[...], wpa_ref[...], preferred_element_type=F32)
    yb = jnp.dot(ob_ref[...], wpb_ref[...], preferred_element_type=F32)
    o_ref[...] = (_sigmoid(ga_ref[...].astype(F32)) * ya + _sigmoid(gb_ref[...].astype(F32)) * yb).astype(BF16)


def _mix(ha, ob, wpa, wpb, u, *, tm, tn, col, name):
    rows, ka = ha.shape
    kb = ob.shape[1]
    n = wpa.shape[1]
    return pl.pallas_call(
        _mix_kernel,
        grid=(rows // tm, n // tn),
        in_specs=[pl.BlockSpec((tm, ka), lambda i, j: (i, 0)),
                  pl.BlockSpec((tm, kb), lambda i, j: (i, 0)),
                  pl.BlockSpec((ka, tn), lambda i, j: (0, j)),
                  pl.BlockSpec((kb, tn), lambda i, j: (0, j)),
                  pl.BlockSpec((tm, tn), lambda i, j: (i, col["ga"] // tn + j)),
                  pl.BlockSpec((tm, tn), lambda i, j: (i, col["gb"] // tn + j))],
        out_specs=pl.BlockSpec((tm, tn), lambda i, j: (i, j)),
        out_shape=jax.ShapeDtypeStruct((rows, n), BF16),
        compiler_params=_params("parallel", "arbitrary"),
        name=name,
    )(ha, ob, wpa, wpb, u, u)


def _out_proj_kernel(mix_ref, w_ref, x_ref, o_ref):
    o_ref[...] = x_ref[...] + jnp.dot(mix_ref[...], w_ref[...], preferred_element_type=F32)


def _out_proj(mix, w, x, *, tm, tn, name):
    rows, k = mix.shape
    n = w.shape[1]
    return pl.pallas_call(
        _out_proj_kernel,
        grid=(rows // tm, n // tn),
        in_specs=[pl.BlockSpec((tm, k), lambda i, j: (i, 0)),
                  pl.BlockSpec((k, tn), lambda i, j: (0, j)),
                  pl.BlockSpec((tm, tn), lambda i, j: (i, j))],
        out_specs=pl.BlockSpec((tm, tn), lambda i, j: (i, j)),
        out_shape=jax.ShapeDtypeStruct((rows, n), F32),
        compiler_params=_params("parallel", "arbitrary"),
        name=name,
    )(mix, w, x)


def _ple_kernel(xf_ref, g_ref, wpg_ref, p_ref, wple_ref, x_ref, o_ref, h_ref):
    @pl.when(pl.program_id(1) == 0)
    def _():
        h_ref[...] = _rms(xf_ref[...], g_ref[...]).astype(BF16)

    gate = _sigmoid(jnp.dot(h_ref[...], wpg_ref[...], preferred_element_type=F32))
    emb = jnp.dot(p_ref[...].astype(BF16), wple_ref[...], preferred_element_type=F32)
    o_ref[...] = x_ref[...] + emb * gate


def _ple(x, g, wpg, p, wple, *, tm, tn, name):
    rows, d = x.shape
    n = wpg.shape[1]
    pd = p.shape[1]
    return pl.pallas_call(
        _ple_kernel,
        grid=(rows // tm, n // tn),
        in_specs=[pl.BlockSpec((tm, d), lambda i, j: (i, 0)),
                  pl.BlockSpec((1, d), lambda i, j: (0, 0)),
                  pl.BlockSpec((d, tn), lambda i, j: (0, j)),
                  pl.BlockSpec((tm, pd), lambda i, j: (i, 0)),
                  pl.BlockSpec((pd, tn), lambda i, j: (0, j)),
                  pl.BlockSpec((tm, tn), lambda i, j: (i, j))],
        out_specs=pl.BlockSpec((tm, tn), lambda i, j: (i, j)),
        out_shape=jax.ShapeDtypeStruct((rows, n), F32),
        scratch_shapes=[pltpu.VMEM((tm, d), BF16)],
        compiler_params=_params("parallel", "arbitrary"),
        name=name,
    )(x, g, wpg, p, wple, x)


def _final_norm_kernel(x_ref, g_ref, o_ref):
    o_ref[...] = _rms(x_ref[...], g_ref[...])


def _final_norm(x, g, *, row0, nrows, tm, name):
    d = x.shape[1]
    blk0 = row0 // tm
    return pl.pallas_call(
        _final_norm_kernel,
        grid=(nrows // tm,),
        in_specs=[pl.BlockSpec((tm, d), lambda i: (blk0 + i, 0)), pl.BlockSpec((1, d), lambda i: (0, 0))],
        out_specs=pl.BlockSpec((tm, d), lambda i: (i, 0)),
        out_shape=jax.ShapeDtypeStruct((nrows, d), F32),
        compiler_params=_params("parallel"),
        name=name,
    )(x, g)


def _rope_tables(pos, rope):
    half = rope // 2
    inv = ROPE_THETA ** (-jnp.arange(half, dtype=F32) / half)
    ang = pos[:, None] * inv[None, :]
    cos, sin, zero = jnp.cos(ang), jnp.sin(ang), jnp.zeros_like(ang)
    reps = LANES // rope
    cos_t = jnp.tile(jnp.concatenate([cos, cos], axis=1), (1, reps))
    sin_lo = jnp.tile(jnp.concatenate([-sin, zero], axis=1), (1, reps))
    sin_hi = jnp.tile(jnp.concatenate([zero, sin], axis=1), (1, reps))
    return cos_t, sin_lo, sin_hi


def kernel(x_prompt, x_sample, p_prompt, p_sample, cache_ckv, cache_kpe, state_C, state_n, state_m,
           page_table, g_in, w_in, b_i, b_f, g_q, w_uq, g_kv, w_uk, w_uv, g_head, w_pa, w_pb,
           w_out, w_ple, g_ple, w_pg, g_final):
    B, S, D = x_prompt.shape
    Bd, T, _ = x_sample.shape
    depth = g_in.shape[0]
    H = b_i.shape[1]
    dv = D // H
    dk = dv // 2
    hq, hv = H * dk, H * dv
    rq, rkv = g_q.shape[1], g_kv.shape[1]
    HB = w_uk.shape[2]
    nope, vdim = w_uk.shape[3], w_uv.shape[3]
    rope = w_uq.shape[3] - nope
    wb = HB * vdim
    page = cache_ckv.shape[2]
    past_len = page_table.shape[1] * page
    mla_scale = float((nope + rope) ** -0.5)
    assert T <= SAMPLE_PAD and LANES % rope == 0 and rope + 2 * H <= LANES
    assert hv == D and wb == D and rq == rkv and (7 * D) % rq == 0

    n_p, n_s = B * S, Bd * SAMPLE_PAD
    rows = n_p + n_s
    tm = _pick_tile(math.gcd(n_p, n_s), (512, 256, 128, 64, 32, 16, 8))
    tm_in = _pick_tile(math.gcd(n_p, n_s), (INPROJ_TM, 512, 256, 128, 64, 32, 16, 8))

    pad_t = ((0, 0), (0, SAMPLE_PAD - T), (0, 0))
    x = jnp.concatenate([x_prompt.reshape(n_p, D), jnp.pad(x_sample, pad_t).reshape(n_s, D)], axis=0)
    pos = jnp.concatenate([jnp.tile(jnp.arange(S, dtype=F32), B),
                           jnp.tile(past_len + jnp.arange(SAMPLE_PAD, dtype=F32), Bd)])
    tables = _rope_tables(pos, rope)
    cache_kpe_t = jnp.swapaxes(cache_kpe, 2, 3)

    col = {"v": 0, "o": D, "za": 2 * D, "zb": 3 * D, "ga": 4 * D, "gb": 5 * D,
           "q": 6 * D, "k": 6 * D + hq, "cq": 7 * D, "ckv": 7 * D + rq}
    n_main = 7 * D + rq + rkv
    sizes = (hq, hq, hv, H, H, hv, hv, rq, rkv, rope, wb, D, D)
    starts = np.concatenate([[0], np.cumsum(sizes)])
    seg = {name: (int(starts[i]), int(starts[i + 1])) for i, name in enumerate(
        ("q", "k", "v", "i", "f", "o", "za", "cq", "ckv", "kr", "zb", "ga", "gb"))}
    main_order = ("v", "o", "za", "zb", "ga", "gb", "q", "k", "cq", "ckv")
    gate_lane = rope
    tn_main = _pick_tile(n_main, (1024, 512, 256, 128))
    tn_d = _pick_tile(D, (1024, 512, 256, 128))

    xs = x
    c_sample = None
    outs = {k: [] for k in ("ckv_p", "kpe_p", "C_p", "n_p", "m_p", "ckv_s", "kpe_s", "n_s", "m_s")}
    for l in range(depth):
        wl = w_in[l]
        w_main = jnp.concatenate([wl[:, seg[n][0]:seg[n][1]] for n in main_order], axis=1).astype(BF16)
        w_small = jnp.concatenate(
            [wl[:, seg[n][0]:seg[n][1]] for n in ("kr", "i", "f")]
            + [jnp.zeros((D, LANES - rope - 2 * H), F32)], axis=1).astype(BF16)
        gate_bias = jnp.concatenate([jnp.zeros((rope,), F32), b_i[l], b_f[l],
                                     jnp.zeros((LANES - rope - 2 * H,), F32)]).reshape(1, LANES)
        wq = w_uq[l]
        w_q = jnp.concatenate([wq[:, :, :nope].reshape(rq, HB * nope),
                               wq[:, :, nope:].reshape(rq, HB * rope)], axis=1).astype(BF16)
        wuk = jnp.transpose(w_uk[l], (1, 2, 0)).astype(BF16)
        wuv = jnp.transpose(w_uv[l], (1, 0, 2)).astype(BF16)

        g_in_l = g_in[l].reshape(1, D)
        u = _norm_matmul(xs, g_in_l, w_main, tm=tm_in, tn=tn_main, out_dtype=BF16, name=f"inproj_main_{l}")
        us = _norm_matmul(xs, g_in_l, w_small, tm=tm_in, tn=LANES, out_dtype=F32, name=f"inproj_small_{l}")

        g_head_l = g_head[l].reshape(1, hv)
        Lp = _pick_tile(S, (MLSTM_CHUNK, 128, 64, 32, 16, 8))
        common = dict(rows_total=rows, H=H, dk=dk, dv=dv, col=col, gate_lane=gate_lane)
        ha, C_p, nn_p, m_p = _mlstm(u, us, gate_bias, g_head_l, None, None, None, layer=0, c_layers=1,
                                    row0=0, B=B, S=S, L=Lp, t_valid=Lp, name=f"mlstm_prompt_{l}", **common)
        ha, c_sample, nn_s, m_s = _mlstm(u, us, gate_bias, g_head_l, (state_C, state_n, state_m), ha, c_sample,
                                         layer=l, c_layers=depth, row0=n_p, B=Bd, S=SAMPLE_PAD, L=SAMPLE_PAD,
                                         t_valid=T, name=f"mlstm_sample_{l}", **common)

        qn, qpe, ckv, ckv_b, kpe, kpe_b = _mla_prep(
            u, us, tables, g_q[l].reshape(1, rq), g_kv[l].reshape(1, rkv), w_q, tm=tm, rq=rq, rkv=rkv,
            col=col, n_nope=HB * nope, n_pe=HB * rope, rope=rope, scale=mla_scale, name=f"mla_prep_{l}")
        kpe_b = kpe_b[:, :rope]
        qpe_h = jnp.transpose(qpe.reshape(rows, HB, rope), (1, 0, 2))
        tq = _pick_tile(S, (ATTN_TQ, 64, 32, 16, 8))
        tk = _pick_tile(S, (ATTN_TK, 256, 128))
        ckv_t = jnp.transpose(ckv_b[:n_p].reshape(n_p // tk, tk, rkv), (0, 2, 1))
        ob = _attn_prompt(qn, qpe_h, ckv_b, ckv_t, kpe_b, wuk, wuv, u, rows_total=rows, B=B, S=S, H=HB,
                          tq=tq, tk=tk, col=col, scale=mla_scale, name=f"attn_prompt_{l}")

        ql_s = _qlat(qn, wuk, row0=n_p, rows=n_s, scale=mla_scale, name=f"qlat_sample_{l}")
        to_bt = lambda a: jnp.transpose(a.reshape(HB, Bd, SAMPLE_PAD, -1)[:, :, :T], (1, 0, 2, 3)).reshape(
            Bd, HB * T, a.shape[-1])
        ol_s = _attn_sample(page_table, to_bt(ql_s), to_bt(qpe_h[:, n_p:]),
                            ckv_b[n_p:].reshape(Bd, SAMPLE_PAD, rkv), kpe_b[n_p:].reshape(Bd, SAMPLE_PAD, rope),
                            cache_ckv, cache_kpe_t, layer=l, T=T,
                            P=_pick_tile(page_table.shape[1], (PAGES_PER_STEP, 16, 8, 4, 2, 1)),
                            name=f"attn_sample_{l}")
        ol_h = jnp.pad(jnp.transpose(ol_s.reshape(Bd, HB, T, rkv), (1, 0, 2, 3)),
                       ((0, 0), (0, 0), (0, SAMPLE_PAD - T), (0, 0))).reshape(HB, n_s, rkv)
        ob = _ouv(ol_h, wuv, u, ob, row0=n_p, rows=n_s, zb_col=col["zb"], name=f"ouv_sample_{l}")

        mix = _mix(ha, ob, w_pa[l].astype(BF16), w_pb[l].astype(BF16), u, tm=tm_in, tn=tn_d, col=col, name=f"mix_{l}")
        x1 = _out_proj(mix, w_out[l].astype(BF16), xs, tm=tm_in, tn=tn_d, name=f"out_proj_{l}")
        p_all = jnp.concatenate([p_prompt[l].reshape(n_p, -1),
                                 jnp.pad(p_sample[l], pad_t).reshape(n_s, -1)], axis=0)
        xs = _ple(x1, g_ple[l].reshape(1, D), w_pg[l].astype(BF16), p_all, w_ple[l].astype(BF16),
                  tm=tm_in, tn=tn_d, name=f"ple_{l}")

        unpad = lambda a: a[n_p:].reshape(Bd, SAMPLE_PAD, -1)[:, :T]
        outs["ckv_p"].append(ckv[:n_p].reshape(B, S, rkv))
        outs["kpe_p"].append(kpe[:n_p, :rope].reshape(B, S, rope))
        outs["C_p"].append(C_p[0]); outs["n_p"].append(nn_p); outs["m_p"].append(m_p)
        outs["ckv_s"].append(unpad(ckv))
        outs["kpe_s"].append(unpad(kpe)[:, :, :rope])
        outs["n_s"].append(nn_s); outs["m_s"].append(m_s)

    g_fin = g_final.reshape(1, D)
    y_prompt = _final_norm(xs, g_fin, row0=0, nrows=n_p, tm=tm, name="final_norm_prompt").reshape(B, S, D)
    y_sample = _final_norm(xs, g_fin, row0=n_p, nrows=n_s, tm=tm, name="final_norm_sample").reshape(
        Bd, SAMPLE_PAD, D)[:, :T]
    st = lambda k: jnp.stack(outs[k])
    return (y_prompt, y_sample, st("ckv_p"), st("kpe_p"), st("C_p"), st("n_p"), st("m_p"),
            st("ckv_s"), st("kpe_s"), c_sample, st("n_s"), st("m_s"))
```

```python
import functools
import math

import jax
import jax.numpy as jnp
import numpy as np
from jax import lax
from jax.experimental import pallas as pl
from jax.experimental.pallas import tpu as pltpu

EPS = 1e-6
ROPE_THETA = 10000.0
F32 = jnp.float32
BF16 = jnp.bfloat16

LANES = 128
SUBLANES = 8
MXU_COLS = 256
SAMPLE_PAD = SUBLANES
VMEM_LIMIT = 56 * 1024 * 1024
MLSTM_CHUNK = 256
MLSTM_SEQS_PER_STEP = 4
ATTN_TQ = 128
ATTN_TK = 512
ATTN_CHUNK = 2 * MXU_COLS
PAGES_PER_STEP = 32
INPROJ_TM = 1024
OUT_PLE_TM = 256

NT_DIMS = (((1,), (1,)), ((), ()))
TN_DIMS = (((0,), (0,)), ((), ()))


def _params(*sem):
    return pltpu.CompilerParams(dimension_semantics=sem, vmem_limit_bytes=VMEM_LIMIT)


def _pick_tile(n, prefs):
    for t in prefs:
        if n % t == 0:
            return t
    return n


def _sigmoid(x):
    return 1.0 / (1.0 + jnp.exp(-x))


def _log_sigmoid(x):
    return jnp.minimum(x, 0.0) - jnp.log(1.0 + jnp.exp(-jnp.abs(x)))


def _rms(x, g):
    return x * lax.rsqrt(jnp.mean(x * x, axis=-1, keepdims=True) + EPS) * g


_ANY = pl.BlockSpec(memory_space=pl.ANY)


def _norm_matmul_kernel(x_ref, g_ref, w_ref, ws_ref, o_ref, os_ref, h_ref):
    @pl.when(pl.program_id(1) == 0)
    def _():
        h_ref[...] = _rms(x_ref[...], g_ref[...]).astype(BF16)
        os_ref[...] = jnp.dot(h_ref[...], ws_ref[...], preferred_element_type=F32)

    o_ref[...] = jnp.dot(h_ref[...], w_ref[...], preferred_element_type=F32).astype(o_ref.dtype)


def _norm_matmul(x, g, w, w_small, *, tm, tn, name):
    m, k = x.shape
    n, ns = w.shape[1], w_small.shape[1]
    return pl.pallas_call(
        _norm_matmul_kernel,
        grid=(m // tm, n // tn),
        in_specs=[pl.BlockSpec((tm, k), lambda i, j: (i, 0)),
                  pl.BlockSpec((1, k), lambda i, j: (0, 0)),
                  pl.BlockSpec((k, tn), lambda i, j: (0, j)),
                  pl.BlockSpec((k, ns), lambda i, j: (0, 0))],
        out_specs=[pl.BlockSpec((tm, tn), lambda i, j: (i, j)),
                   pl.BlockSpec((tm, ns), lambda i, j: (i, 0))],
        out_shape=[jax.ShapeDtypeStruct((m, n), BF16), jax.ShapeDtypeStruct((m, ns), F32)],
        scratch_shapes=[pltpu.VMEM((tm, k), BF16)],
        compiler_params=_params("parallel", "arbitrary"),
        name=name,
    )(x, g, w, w_small)


def _mlstm_kernel(*refs, H, dk, dv, L, BB, t_valid, nc, has_init, n_alias, gate_lane):
    q_ref, k_ref, v_ref, o_ref, z_ref, s_ref, bias_ref, gh_ref = refs[:8]
    pos = 8
    if has_init:
        c0_ref, n0_ref, m0_ref = refs[pos:pos + 3]
        pos += 3
    pos += n_alias
    h_out, c_out, n_out, m_out, c_s, n_s, m_s = refs[pos:]
    c = pl.program_id(1)

    @pl.when(c == 0)
    def _():
        if has_init:
            c_s[...] = c0_ref[0]
            n_s[...] = n0_ref[0]
            m_s[...] = m0_ref[0]
        else:
            c_s[...] = jnp.zeros_like(c_s)
            n_s[...] = jnp.zeros_like(n_s)
            m_s[...] = jnp.zeros_like(m_s)

    row = lax.broadcasted_iota(jnp.int32, (L, L), 0)
    col = lax.broadcasted_iota(jnp.int32, (L, L), 1)
    causal = col <= row
    eye = col == row
    tok = lax.broadcasted_iota(jnp.int32, (L, 1), 0)
    head_row = lax.broadcasted_iota(jnp.int32, (H, dk), 0)
    head_lane = lax.broadcasted_iota(jnp.int32, (1, H), 1)
    scale = dk ** -0.5

    for bb, h in [(bb, h) for bb in range(BB) for h in range(H)]:
        r = slice(bb * L, (bb + 1) * L)
        if h == 0:
            n_all, m_all = n_s[bb], m_s[bb]
            n_acc, m_acc = n_all, m_all
            gates = s_ref[r, :] + bias_ref[...]
            csum = jnp.dot(causal.astype(F32), _log_sigmoid(gates), preferred_element_type=F32,
                           precision=lax.Precision.HIGHEST)
        qb = q_ref[r, h * dk:(h + 1) * dk]
        kb = k_ref[r, h * dk:(h + 1) * dk]
        vb = v_ref[r, h * dv:(h + 1) * dv]
        icol = gates[:, gate_lane + h:gate_lane + h + 1]
        bcol = csum[:, gate_lane + H + h:gate_lane + H + h + 1]
        wcol = icol - bcol
        wrow = jnp.sum(jnp.where(eye, wcol, 0.0), axis=0, keepdims=True)
        d = jnp.where(causal, bcol + wrow, -jnp.inf)
        m_prev = m_all[:, h:h + 1]
        inter = bcol + m_prev
        m_t = jnp.maximum(inter, jnp.max(d, axis=-1, keepdims=True))
        s_inter = jnp.exp(inter - m_t) * scale
        a = lax.dot_general(qb, kb, NT_DIMS, preferred_element_type=F32) * (jnp.exp(d - m_t) * scale)
        c_old = c_s[bb, h]
        num = (s_inter * lax.dot_general(qb, c_old.astype(BF16), NT_DIMS, preferred_element_type=F32)
               + jnp.dot(a.astype(BF16), vb, preferred_element_type=F32))
        n_old = n_all[h:h + 1, :]
        den = (s_inter * jnp.sum(qb.astype(F32) * n_old, axis=-1, keepdims=True)
               + jnp.sum(a, axis=-1, keepdims=True))
        hval = num / jnp.maximum(jnp.abs(den), jnp.exp(-m_t))

        m_new = m_t[t_valid - 1:t_valid, :]
        b_last = bcol[t_valid - 1:t_valid, :]
        g_state = jnp.exp(b_last + m_prev - m_new)
        g_tok = jnp.exp(wcol + (b_last - m_new))
        if t_valid < L:
            g_tok = jnp.where(tok < t_valid, g_tok, 0.0)
        vg = (vb.astype(F32) * g_tok).astype(BF16)
        c_s[bb, h] = g_state * c_old + lax.dot_general(vg, kb, TN_DIMS, preferred_element_type=F32)
        n_new = g_state * n_old + jnp.sum(kb.astype(F32) * g_tok, axis=0, keepdims=True)
        n_acc = jnp.where(head_row == h, n_new, n_acc)
        m_acc = jnp.where(head_lane == h, m_new, m_acc)
        if h == H - 1:
            n_s[bb] = n_acc
            m_s[bb] = m_acc

        hg = hval * _sigmoid(o_ref[r, h * dv:(h + 1) * dv].astype(F32))
        hn = _rms(hg, gh_ref[:, h * dv:(h + 1) * dv])
        z = z_ref[r, h * dv:(h + 1) * dv].astype(F32)
        h_out[r, h * dv:(h + 1) * dv] = (hn * (z * _sigmoid(z))).astype(BF16)

    @pl.when(c == nc - 1)
    def _():
        c_out[0] = c_s[...]
        n_out[...] = n_s[...]
        m_out[...] = m_s[...]


def _mlstm(u, us, gate_bias, g_head, init, h_buf, c_buf, *, layer, c_layers, rows_total, row0, B, S, L, t_valid,
           H, dk, dv, col, gate_lane, name):
    nc = S // L
    BB = _pick_tile(B, (MLSTM_SEQS_PER_STEP, 2, 1)) if nc == 1 else 1
    rb = BB * L
    assert row0 % rb == 0
    blk0 = row0 // rb
    hq, hv = H * dk, H * dv

    def rows(width, off):
        return pl.BlockSpec((rb, width), lambda b, c: (blk0 + b * nc + c, off // width))

    in_specs = [rows(hq, col["q"]), rows(hq, col["k"]), rows(hv, col["v"]), rows(hv, col["o"]),
                rows(hv, col["za"]), rows(LANES, 0),
                pl.BlockSpec((1, LANES), lambda b, c: (0, 0)),
                pl.BlockSpec((1, hv), lambda b, c: (0, 0))]
    args = [u, u, u, u, u, us, gate_bias, g_head]
    if init is not None:
        c0, n0, m0 = init
        in_specs += [pl.BlockSpec((1, BB, H, dv, dk), lambda b, c: (layer, b, 0, 0, 0)),
                     pl.BlockSpec((1, BB, H, dk), lambda b, c: (layer, b, 0, 0)),
                     pl.BlockSpec((1, BB, 1, H), lambda b, c: (layer, b, 0, 0))]
        args += [c0, n0, m0.reshape(m0.shape[0], B, 1, H)]
    aliases = {}
    for buf, out_idx in ((h_buf, 0), (c_buf, 1)):
        if buf is not None:
            aliases[len(args)] = out_idx
            in_specs.append(_ANY)
            args.append(buf)
    kern = functools.partial(_mlstm_kernel, H=H, dk=dk, dv=dv, L=L, BB=BB, t_valid=t_valid, nc=nc,
                             has_init=init is not None, n_alias=len(aliases), gate_lane=gate_lane)
    h_out, c_out, n_out, m_out = pl.pallas_call(
        kern,
        grid=(B // BB, nc),
        in_specs=in_specs,
        out_specs=[pl.BlockSpec((rb, hv), lambda b, c: (blk0 + b * nc + c, 0)),
                   pl.BlockSpec((1, BB, H, dv, dk), lambda b, c: (layer, b, 0, 0, 0)),
                   pl.BlockSpec((BB, H, dk), lambda b, c: (b, 0, 0)),
                   pl.BlockSpec((BB, 1, H), lambda b, c: (b, 0, 0))],
        out_shape=[jax.ShapeDtypeStruct((rows_total, hv), BF16),
                   jax.ShapeDtypeStruct((c_layers, B, H, dv, dk), F32),
                   jax.ShapeDtypeStruct((B, H, dk), F32),
                   jax.ShapeDtypeStruct((B, 1, H), F32)],
        scratch_shapes=[pltpu.VMEM((BB, H, dv, dk), F32), pltpu.VMEM((BB, H, dk), F32),
                        pltpu.VMEM((BB, 1, H), F32)],
        input_output_aliases=aliases,
        compiler_params=_params("parallel", "arbitrary"),
        name=name,
    )(*args)
    return h_out, c_out, n_out, m_out.reshape(B, H)


def _rope_lanes(x, cos_t, sin_lo, sin_hi):
    return x * cos_t + pltpu.roll(x, 96, 1) * sin_lo + pltpu.roll(x, 32, 1) * sin_hi


def _mla_prep_kernel(cq_ref, ckv_ref, s_ref, cos_ref, slo_ref, shi_ref, gq_ref, gkv_ref, wq_ref,
                     qn_out, qpe_out, ckv_out, ckvb_out, kpe_out, kpeb_out, *, n_nope, n_pe, rope, scale):
    cos_t, sin_lo, sin_hi = cos_ref[...], slo_ref[...], shi_ref[...]
    cqn = _rms(cq_ref[...].astype(F32), gq_ref[...]).astype(BF16)
    qf = jnp.dot(cqn, wq_ref[...], preferred_element_type=F32)
    qn_out[...] = qf[:, :n_nope].astype(BF16)
    for c in range(n_pe // LANES):
        blk = qf[:, n_nope + c * LANES:n_nope + (c + 1) * LANES]
        roped = (_rope_lanes(blk, cos_t, sin_lo, sin_hi) * scale).astype(BF16)
        for g in range(LANES // rope):
            qpe_out[c * (LANES // rope) + g] = roped[:, g * rope:(g + 1) * rope]
    ckv = _rms(ckv_ref[...].astype(F32), gkv_ref[...])
    ckv_out[...] = ckv
    ckvb_out[...] = ckv.astype(BF16)
    lane = lax.broadcasted_iota(jnp.int32, s_ref.shape, 1)
    kpe = jnp.where(lane < rope, _rope_lanes(s_ref[...], cos_t, sin_lo, sin_hi), 0.0)
    kpe_out[...] = kpe
    kpeb_out[...] = kpe.astype(BF16)


def _mla_prep(u, us, tables, g_q, g_kv, w_q, *, tm, rq, rkv, col, n_nope, n_pe, rope, scale, name):
    rows = u.shape[0]
    cos_t, sin_lo, sin_hi = tables
    tok = lambda width, off: pl.BlockSpec((tm, width), lambda i: (i, off // width))
    const = lambda a: pl.BlockSpec(a.shape, lambda i: (0,) * a.ndim)
    kern = functools.partial(_mla_prep_kernel, n_nope=n_nope, n_pe=n_pe, rope=rope, scale=scale)
    return pl.pallas_call(
        kern,
        grid=(rows // tm,),
        in_specs=[tok(rq, col["cq"]), tok(rkv, col["ckv"]), tok(LANES, 0), tok(LANES, 0), tok(LANES, 0),
                  tok(LANES, 0), const(g_q), const(g_kv), const(w_q)],
        out_specs=[tok(n_nope, 0), pl.BlockSpec((n_pe // rope, tm, rope), lambda i: (0, i, 0)),
                   tok(rkv, 0), tok(rkv, 0), tok(LANES, 0), tok(LANES, 0)],
        out_shape=[jax.ShapeDtypeStruct((rows, n_nope), BF16), jax.ShapeDtypeStruct((n_pe // rope, rows, rope), BF16),
                   jax.ShapeDtypeStruct((rows, rkv), F32), jax.ShapeDtypeStruct((rows, rkv), BF16),
                   jax.ShapeDtypeStruct((rows, LANES), F32), jax.ShapeDtypeStruct((rows, LANES), BF16)],
        compiler_params=_params("parallel"),
        name=name,
    )(u, u, us, cos_t, sin_lo, sin_hi, g_q, g_kv, w_q)


def _attn_prompt_kernel(qn_ref, qpe_ref, ckv_ref, ckvt_ref, kpe_ref, wuk_ref, wuv_ref, zb_ref, o_ref,
                        qlat_s, m_s, l_s, acc_s, *, H, tq, tk, nope, vdim, scale, chunk):
    qi = pl.program_id(1)
    cols = H * tq
    for h in range(H):
        ql = jnp.dot(qn_ref[:, h * nope:(h + 1) * nope], wuk_ref[h], preferred_element_type=F32)
        qlat_s[h * tq:(h + 1) * tq, :] = (ql * scale).astype(BF16)
    m_s[...] = jnp.full(m_s.shape, -jnp.inf, F32)
    l_s[...] = jnp.zeros_like(l_s)
    acc_s[...] = jnp.zeros_like(acc_s)

    def step(kb, masked):
        ks = pl.multiple_of(kb * tk, tk)
        kc = ckv_ref[pl.ds(ks, tk), :]
        kp = kpe_ref[pl.ds(ks, tk), :]
        kct = ckvt_ref[kb]
        def scores(c0):
            qp = qpe_ref[c0 // tq:(c0 + chunk) // tq].reshape(chunk, qpe_ref.shape[-1])
            return (lax.dot_general(kc, qlat_s[c0:c0 + chunk, :], NT_DIMS, preferred_element_type=F32)
                    + lax.dot_general(kp, qp, NT_DIMS, preferred_element_type=F32))

        s_next = scores(0)
        for c0 in range(0, cols, chunk):
            sl = slice(c0, c0 + chunk)
            s = s_next
            if c0 + chunk < cols:
                s_next = scores(c0 + chunk)
            if masked:
                k_pos = ks + lax.broadcasted_iota(jnp.int32, (tk, chunk), 0)
                q_pos = qi * tq + (lax.broadcasted_iota(jnp.int32, (tk, chunk), 1) & (tq - 1))
                s = jnp.where(k_pos <= q_pos, s, -jnp.inf)
            m_old = m_s[:, sl]
            m_new = jnp.maximum(m_old, jnp.max(s, axis=0, keepdims=True))
            alpha = jnp.exp(m_old - m_new)
            p = jnp.exp(s - m_new)
            l_s[:, sl] = alpha * l_s[:, sl] + jnp.sum(p, axis=0, keepdims=True)
            acc_s[:, sl] = alpha * acc_s[:, sl] + jnp.dot(kct, p.astype(BF16), preferred_element_type=F32)
            m_s[:, sl] = m_new

    n_full = (qi * tq) // tk

    def body(kb, carry):
        step(kb, False)
        return carry

    lax.fori_loop(0, n_full, body, 0)
    step(n_full, True)

    for h in range(H):
        sl = slice(h * tq, (h + 1) * tq)
        o_t = (acc_s[:, sl] / l_s[:, sl]).astype(BF16)
        ob = lax.dot_general(o_t, wuv_ref[h], TN_DIMS, preferred_element_type=F32)
        z = zb_ref[:, h * vdim:(h + 1) * vdim].astype(F32)
        o_ref[:, h * vdim:(h + 1) * vdim] = (ob * (z * _sigmoid(z))).astype(BF16)


def _attn_prompt(qn, qpe_h, ckv_b, ckv_t, kpe_b, wuk, wuv, u, *, rows_total, B, S, H, tq, tk, col, scale, name):
    nq = S // tq
    nope, rkv = wuk.shape[1], wuk.shape[2]
    vdim = wuv.shape[2]
    rp = qpe_h.shape[-1]
    hv = H * vdim
    chunk = _pick_tile(H * tq, (ATTN_CHUNK, 2 * tq, tq))
    kern = functools.partial(_attn_prompt_kernel, H=H, tq=tq, tk=tk, nope=nope, vdim=vdim, scale=scale, chunk=chunk)
    once = pl.Buffered(1)
    return pl.pallas_call(
        kern,
        grid=(B, nq),
        in_specs=[pl.BlockSpec((tq, H * nope), lambda b, q: (b * nq + q, 0)),
                  pl.BlockSpec((H, tq, rp), lambda b, q: (0, b * nq + q, 0)),
                  pl.BlockSpec((S, rkv), lambda b, q: (b, 0), pipeline_mode=once),
                  pl.BlockSpec((S // tk, rkv, tk), lambda b, q: (b, 0, 0), pipeline_mode=once),
                  pl.BlockSpec((S, rp), lambda b, q: (b, 0), pipeline_mode=once),
                  pl.BlockSpec(wuk.shape, lambda b, q: (0, 0, 0), pipeline_mode=once),
                  pl.BlockSpec(wuv.shape, lambda b, q: (0, 0, 0), pipeline_mode=once),
                  pl.BlockSpec((tq, hv), lambda b, q: (b * nq + q, col["zb"] // hv))],
        out_specs=pl.BlockSpec((tq, hv), lambda b, q: (b * nq + q, 0)),
        out_shape=jax.ShapeDtypeStruct((rows_total, hv), BF16),
        scratch_shapes=[pltpu.VMEM((H * tq, rkv), BF16), pltpu.VMEM((1, H * tq), F32),
                        pltpu.VMEM((1, H * tq), F32), pltpu.VMEM((rkv, H * tq), F32)],
        compiler_params=_params("parallel", "arbitrary"),
        name=name,
    )(qn, qpe_h, ckv_b, ckv_t, kpe_b, wuk, wuv, u)


def _qlat_kernel(qn_ref, wuk_ref, o_ref, *, scale):
    o_ref[0] = (jnp.dot(qn_ref[...], wuk_ref[0], preferred_element_type=F32) * scale).astype(BF16)


def _qlat(qn, wuk, *, row0, rows, scale, name):
    H, nope, rkv = wuk.shape
    assert row0 % rows == 0
    return pl.pallas_call(
        functools.partial(_qlat_kernel, scale=scale),
        grid=(H,),
        in_specs=[pl.BlockSpec((rows, nope), lambda h: (row0 // rows, h)),
                  pl.BlockSpec((1, nope, rkv), lambda h: (h, 0, 0))],
        out_specs=pl.BlockSpec((1, rows, rkv), lambda h: (h, 0, 0)),
        out_shape=jax.ShapeDtypeStruct((H, rows, rkv), BF16),
        compiler_params=_params("parallel"),
        name=name,
    )(qn, wuk)


def _ouv_kernel(ol_ref, wuv_ref, zb_ref, buf_ref, o_ref):
    del buf_ref
    ob = jnp.dot(ol_ref[0], wuv_ref[0], preferred_element_type=F32)
    z = zb_ref[...].astype(F32)
    o_ref[...] = (ob * (z * _sigmoid(z))).astype(BF16)


def _ouv(o_lat, wuv, u, ob_buf, *, row0, rows, zb_col, name):
    H, rkv, vdim = wuv.shape
    return pl.pallas_call(
        _ouv_kernel,
        grid=(H,),
        in_specs=[pl.BlockSpec((1, rows, rkv), lambda h: (h, 0, 0)),
                  pl.BlockSpec((1, rkv, vdim), lambda h: (h, 0, 0)),
                  pl.BlockSpec((rows, vdim), lambda h: (row0 // rows, zb_col // vdim + h)),
                  _ANY],
        out_specs=pl.BlockSpec((rows, vdim), lambda h: (row0 // rows, h)),
        out_shape=jax.ShapeDtypeStruct(ob_buf.shape, BF16),
        input_output_aliases={3: 0},
        compiler_params=_params("parallel"),
        name=name,
    )(o_lat, wuv, u, ob_buf)


def _attn_sample_kernel(pt_ref, ql_ref, qp_ref, cn_ref, kn_ref, ckv_hbm, kpe_hbm, o_ref,
                        ckv_buf, kpe_buf, sems, m_s, l_s, acc_s, *, layer, P, T, n_batch, n_steps, n_pages):
    page = ckv_buf.shape[1] // P
    b = pl.program_id(0)
    j = pl.program_id(1)
    step = b * n_steps + j
    slot = step % 2

    def page_copies(bb, jj, sl):
        out = []
        for i in range(P):
            pid = pt_ref[bb * n_pages + jj * P + i]
            out.append(pltpu.make_async_copy(ckv_hbm.at[layer, pid], ckv_buf.at[sl, pl.ds(i * page, page)],
                                             sems.at[sl, 0]))
            out.append(pltpu.make_async_copy(kpe_hbm.at[layer, pid], kpe_buf.at[sl, i], sems.at[sl, 1]))
        return out

    @pl.when(step == 0)
    def _():
        for cp in page_copies(b, j, slot):
            cp.start()

    @pl.when(step + 1 < n_batch * n_steps)
    def _():
        wrap = j == n_steps - 1
        for cp in page_copies(jnp.where(wrap, b + 1, b), jnp.where(wrap, 0, j + 1), 1 - slot):
            cp.start()

    for cp in page_copies(b, j, slot):
        cp.wait()

    ql = ql_ref[0]
    qp = qp_ref[0]

    @pl.when(j == 0)
    def _():
        m_s[...] = jnp.full(m_s.shape, -jnp.inf, F32)
        l_s[...] = jnp.zeros_like(l_s)
        acc_s[...] = jnp.zeros_like(acc_s)

    def update(s, v):
        m_old = m_s[...]
        m_new = jnp.maximum(m_old, jnp.max(s, axis=-1, keepdims=True))
        alpha = jnp.exp(m_old - m_new)
        p = jnp.exp(s - m_new)
        l_s[...] = alpha * l_s[...] + jnp.sum(p, axis=-1, keepdims=True)
        acc_s[...] = alpha * acc_s[...] + jnp.dot(p.astype(BF16), v, preferred_element_type=F32)
        m_s[...] = m_new

    kc = ckv_buf[slot].astype(BF16)
    kpt = jnp.concatenate([kpe_buf[slot, i].astype(BF16) for i in range(P)], axis=1)
    update(lax.dot_general(ql, kc, NT_DIMS, preferred_element_type=F32)
           + jnp.dot(qp, kpt, preferred_element_type=F32), kc)

    @pl.when(j == n_steps - 1)
    def _():
        cn = cn_ref[0]
        kn = kn_ref[0]
        s = (lax.dot_general(ql, cn, NT_DIMS, preferred_element_type=F32)
             + lax.dot_general(qp, kn, NT_DIMS, preferred_element_type=F32))
        t_q = lax.broadcasted_iota(jnp.int32, s.shape, 0) & (T - 1)
        t_k = lax.broadcasted_iota(jnp.int32, s.shape, 1)
        update(jnp.where(t_k <= t_q, s, -jnp.inf), cn)
        o_ref[0] = (acc_s[...] / l_s[...]).astype(BF16)


def _attn_sample(page_table, ql, qp, cn, kn, cache_ckv, cache_kpe_t, *, layer, T, P, name):
    Bd, rows, rkv = ql.shape
    rope = qp.shape[-1]
    n_pages = page_table.shape[1]
    page = cache_ckv.shape[2]
    n_steps = n_pages // P
    assert T & (T - 1) == 0 and n_pages % P == 0

    per_b = lambda shape: pl.BlockSpec((1,) + shape, lambda b, j, pt: (b, 0, 0))
    grid_spec = pltpu.PrefetchScalarGridSpec(
        num_scalar_prefetch=1,
        grid=(Bd, n_steps),
        in_specs=[per_b((rows, rkv)), per_b((rows, rope)), per_b((SAMPLE_PAD, rkv)), per_b((SAMPLE_PAD, rope)),
                  _ANY, _ANY],
        out_specs=per_b((rows, rkv)),
        scratch_shapes=[pltpu.VMEM((2, P * page, rkv), F32),
                        pltpu.VMEM((2, P, rope, page), F32),
                        pltpu.SemaphoreType.DMA((2, 2)),
                        pltpu.VMEM((rows, 1), F32), pltpu.VMEM((rows, 1), F32), pltpu.VMEM((rows, rkv), F32)],
    )
    kern = functools.partial(_attn_sample_kernel, layer=layer, P=P, T=T, n_batch=Bd, n_steps=n_steps,
                             n_pages=n_pages)
    return pl.pallas_call(
        kern,
        grid_spec=grid_spec,
        out_shape=jax.ShapeDtypeStruct((Bd, rows, rkv), BF16),
        compiler_params=_params("arbitrary", "arbitrary"),
        name=name,
    )(page_table.reshape(-1), ql, qp, cn, kn, cache_ckv, cache_kpe_t)


def _mix_kernel(ha_ref, ob_ref, wpa_ref, wpb_ref, ga_ref, gb_ref, o_ref):
    ya = jnp.dot(ha_ref[...], wpa_ref[...], preferred_element_type=F32)
    yb = jnp.dot(ob_ref[...], wpb_ref[...], preferred_element_type=F32)
    o_ref[...] = (_sigmoid(ga_ref[...].astype(F32)) * ya + _sigmoid(gb_ref[...].astype(F32)) * yb).astype(BF16)


def _mix(ha, ob, wpa, wpb, u, *, tm, tn, col, name):
    rows, ka = ha.shape
    kb = ob.shape[1]
    n = wpa.shape[1]
    return pl.pallas_call(
        _mix_kernel,
        grid=(rows // tm, n // tn),
        in_specs=[pl.BlockSpec((tm, ka), lambda i, j: (i, 0)),
                  pl.BlockSpec((tm, kb), lambda i, j: (i, 0)),
                  pl.BlockSpec((ka, tn), lambda i, j: (0, j)),
                  pl.BlockSpec((kb, tn), lambda i, j: (0, j)),
                  pl.BlockSpec((tm, tn), lambda i, j: (i, col["ga"] // tn + j)),
                  pl.BlockSpec((tm, tn), lambda i, j: (i, col["gb"] // tn + j))],
        out_specs=pl.BlockSpec((tm, tn), lambda i, j: (i, j)),
        out_shape=jax.ShapeDtypeStruct((rows, n), BF16),
        compiler_params=_params("parallel", "arbitrary"),
        name=name,
    )(ha, ob, wpa, wpb, u, u)


def _out_ple_kernel(mix_ref, wout_ref, x_ref, g_ref, wpg_ref, p_ref, wple_ref, o_ref):
    x1 = x_ref[...] + jnp.dot(mix_ref[...], wout_ref[...], preferred_element_type=F32)
    gate = _sigmoid(jnp.dot(_rms(x1, g_ref[...]).astype(BF16), wpg_ref[...], preferred_element_type=F32))
    emb = jnp.dot(p_ref[...].astype(BF16), wple_ref[...], preferred_element_type=F32)
    o_ref[...] = x1 + emb * gate


def _out_ple(mix, wout, x, g, wpg, p, wple, *, tm, name):
    rows, d = x.shape
    pd = p.shape[1]
    once = pl.Buffered(1)
    row_tile = lambda width: pl.BlockSpec((tm, width), lambda i: (i, 0))
    const = lambda a: pl.BlockSpec(a.shape, lambda i: (0, 0), pipeline_mode=once)
    return pl.pallas_call(
        _out_ple_kernel,
        grid=(rows // tm,),
        in_specs=[row_tile(d), const(wout), row_tile(d), const(g), const(wpg), row_tile(pd), const(wple)],
        out_specs=row_tile(d),
        out_shape=jax.ShapeDtypeStruct((rows, d), F32),
        compiler_params=_params("parallel"),
        name=name,
    )(mix, wout, x, g, wpg, p, wple)


def _final_norm_kernel(x_ref, g_ref, o_ref):
    o_ref[...] = _rms(x_ref[...], g_ref[...])


def _final_norm(x, g, *, row0, nrows, tm, name):
    d = x.shape[1]
    blk0 = row0 // tm
    return pl.pallas_call(
        _final_norm_kernel,
        grid=(nrows // tm,),
        in_specs=[pl.BlockSpec((tm, d), lambda i: (blk0 + i, 0)), pl.BlockSpec((1, d), lambda i: (0, 0))],
        out_specs=pl.BlockSpec((tm, d), lambda i: (i, 0)),
        out_shape=jax.ShapeDtypeStruct((nrows, d), F32),
        compiler_params=_params("parallel"),
        name=name,
    )(x, g)


def _rope_tables(pos, rope):
    half = rope // 2
    inv = ROPE_THETA ** (-jnp.arange(half, dtype=F32) / half)
    ang = pos[:, None] * inv[None, :]
    cos, sin, zero = jnp.cos(ang), jnp.sin(ang), jnp.zeros_like(ang)
    reps = LANES // rope
    cos_t = jnp.tile(jnp.concatenate([cos, cos], axis=1), (1, reps))
    sin_lo = jnp.tile(jnp.concatenate([-sin, zero], axis=1), (1, reps))
    sin_hi = jnp.tile(jnp.concatenate([zero, sin], axis=1), (1, reps))
    return cos_t, sin_lo, sin_hi


def kernel(x_prompt, x_sample, p_prompt, p_sample, cache_ckv, cache_kpe, state_C, state_n, state_m,
           page_table, g_in, w_in, b_i, b_f, g_q, w_uq, g_kv, w_uk, w_uv, g_head, w_pa, w_pb,
           w_out, w_ple, g_ple, w_pg, g_final):
    B, S, D = x_prompt.shape
    Bd, T, _ = x_sample.shape
    depth = g_in.shape[0]
    H = b_i.shape[1]
    dv = D // H
    dk = dv // 2
    hq, hv = H * dk, H * dv
    rq, rkv = g_q.shape[1], g_kv.shape[1]
    HB = w_uk.shape[2]
    nope, vdim = w_uk.shape[3], w_uv.shape[3]
    rope = w_uq.shape[3] - nope
    wb = HB * vdim
    page = cache_ckv.shape[2]
    past_len = page_table.shape[1] * page
    mla_scale = float((nope + rope) ** -0.5)
    assert T <= SAMPLE_PAD and LANES % rope == 0 and rope + 2 * H <= LANES
    assert hv == D and wb == D and rq == rkv and (7 * D) % rq == 0

    n_p, n_s = B * S, Bd * SAMPLE_PAD
    rows = n_p + n_s
    tm = _pick_tile(math.gcd(n_p, n_s), (512, 256, 128, 64, 32, 16, 8))
    tm_in = _pick_tile(math.gcd(n_p, n_s), (INPROJ_TM, 512, 256, 128, 64, 32, 16, 8))

    pad_t = ((0, 0), (0, SAMPLE_PAD - T), (0, 0))
    x = jnp.concatenate([x_prompt.reshape(n_p, D), jnp.pad(x_sample, pad_t).reshape(n_s, D)], axis=0)
    pos = jnp.concatenate([jnp.tile(jnp.arange(S, dtype=F32), B),
                           jnp.tile(past_len + jnp.arange(SAMPLE_PAD, dtype=F32), Bd)])
    tables = _rope_tables(pos, rope)
    cache_kpe_t = jnp.swapaxes(cache_kpe, 2, 3)

    col = {"v": 0, "o": D, "za": 2 * D, "zb": 3 * D, "ga": 4 * D, "gb": 5 * D,
           "q": 6 * D, "k": 6 * D + hq, "cq": 7 * D, "ckv": 7 * D + rq}
    n_main = 7 * D + rq + rkv
    sizes = (hq, hq, hv, H, H, hv, hv, rq, rkv, rope, wb, D, D)
    starts = np.concatenate([[0], np.cumsum(sizes)])
    seg = {name: (int(starts[i]), int(starts[i + 1])) for i, name in enumerate(
        ("q", "k", "v", "i", "f", "o", "za", "cq", "ckv", "kr", "zb", "ga", "gb"))}
    main_order = ("v", "o", "za", "zb", "ga", "gb", "q", "k", "cq", "ckv")
    gate_lane = rope
    tn_main = _pick_tile(n_main, (1024, 512, 256, 128))
    tn_d = _pick_tile(D, (1024, 512, 256, 128))

    xs = x
    c_sample = None
    outs = {k: [] for k in ("ckv_p", "kpe_p", "C_p", "n_p", "m_p", "ckv_s", "kpe_s", "n_s", "m_s")}
    for l in range(depth):
        wl = w_in[l]
        w_main = jnp.concatenate([wl[:, seg[n][0]:seg[n][1]].astype(BF16) for n in main_order], axis=1)
        w_small = jnp.concatenate(
            [wl[:, seg[n][0]:seg[n][1]].astype(BF16) for n in ("kr", "i", "f")]
            + [jnp.zeros((D, LANES - rope - 2 * H), BF16)], axis=1)
        gate_bias = jnp.concatenate([jnp.zeros((rope,), F32), b_i[l], b_f[l],
                                     jnp.zeros((LANES - rope - 2 * H,), F32)]).reshape(1, LANES)
        wq = w_uq[l]
        w_q = jnp.concatenate([wq[:, :, :nope].reshape(rq, HB * nope),
                               wq[:, :, nope:].reshape(rq, HB * rope)], axis=1).astype(BF16)
        wuk = jnp.transpose(w_uk[l], (1, 2, 0)).astype(BF16)
        wuv = jnp.transpose(w_uv[l], (1, 0, 2)).astype(BF16)

        g_in_l = g_in[l].reshape(1, D)
        u, us = _norm_matmul(xs, g_in_l, w_main, w_small, tm=tm_in, tn=tn_main, name=f"inproj_{l}")

        g_head_l = g_head[l].reshape(1, hv)
        Lp = _pick_tile(S, (MLSTM_CHUNK, 128, 64, 32, 16, 8))
        common = dict(rows_total=rows, H=H, dk=dk, dv=dv, col=col, gate_lane=gate_lane)
        ha, C_p, nn_p, m_p = _mlstm(u, us, gate_bias, g_head_l, None, None, None, layer=0, c_layers=1,
                                    row0=0, B=B, S=S, L=Lp, t_valid=Lp, name=f"mlstm_prompt_{l}", **common)
        ha, c_sample, nn_s, m_s = _mlstm(u, us, gate_bias, g_head_l, (state_C, state_n, state_m), ha, c_sample,
                                         layer=l, c_layers=depth, row0=n_p, B=Bd, S=SAMPLE_PAD, L=SAMPLE_PAD,
                                         t_valid=T, name=f"mlstm_sample_{l}", **common)

        qn, qpe_h, ckv, ckv_b, kpe, kpe_b = _mla_prep(
            u, us, tables, g_q[l].reshape(1, rq), g_kv[l].reshape(1, rkv), w_q, tm=tm, rq=rq, rkv=rkv,
            col=col, n_nope=HB * nope, n_pe=HB * rope, rope=rope, scale=mla_scale, name=f"mla_prep_{l}")
        kpe_b = kpe_b[:, :rope]
        tq = _pick_tile(S, (ATTN_TQ, 64, 32, 16, 8))
        tk = _pick_tile(S, (ATTN_TK, 256, 128))
        ckv_t = jnp.transpose(ckv_b[:n_p].reshape(n_p // tk, tk, rkv), (0, 2, 1))
        ob = _attn_prompt(qn, qpe_h, ckv_b, ckv_t, kpe_b, wuk, wuv, u, rows_total=rows, B=B, S=S, H=HB,
                          tq=tq, tk=tk, col=col, scale=mla_scale, name=f"attn_prompt_{l}")

        ql_s = _qlat(qn, wuk, row0=n_p, rows=n_s, scale=mla_scale, name=f"qlat_sample_{l}")
        to_bt = lambda a: jnp.transpose(a.reshape(HB, Bd, SAMPLE_PAD, -1)[:, :, :T], (1, 0, 2, 3)).reshape(
            Bd, HB * T, a.shape[-1])
        ol_s = _attn_sample(page_table, to_bt(ql_s), to_bt(qpe_h[:, n_p:]),
                            ckv_b[n_p:].reshape(Bd, SAMPLE_PAD, rkv), kpe_b[n_p:].reshape(Bd, SAMPLE_PAD, rope),
                            cache_ckv, cache_kpe_t, layer=l, T=T,
                            P=_pick_tile(page_table.shape[1], (PAGES_PER_STEP, 16, 8, 4, 2, 1)),
                            name=f"attn_sample_{l}")
        ol_h = jnp.pad(jnp.transpose(ol_s.reshape(Bd, HB, T, rkv), (1, 0, 2, 3)),
                       ((0, 0), (0, 0), (0, SAMPLE_PAD - T), (0, 0))).reshape(HB, n_s, rkv)
        ob = _ouv(ol_h, wuv, u, ob, row0=n_p, rows=n_s, zb_col=col["zb"], name=f"ouv_sample_{l}")

        mix = _mix(ha, ob, w_pa[l].astype(BF16), w_pb[l].astype(BF16), u, tm=tm_in, tn=tn_d, col=col, name=f"mix_{l}")
        p_all = jnp.concatenate([p_prompt[l].reshape(n_p, -1),
                                 jnp.pad(p_sample[l], pad_t).reshape(n_s, -1)], axis=0)
        xs = _out_ple(mix, w_out[l].astype(BF16), xs, g_ple[l].reshape(1, D), w_pg[l].astype(BF16), p_all,
                      w_ple[l].astype(BF16), tm=_pick_tile(tm, (OUT_PLE_TM,)), name=f"out_ple_{l}")

        unpad = lambda a: a[n_p:].reshape(Bd, SAMPLE_PAD, -1)[:, :T]
        outs["ckv_p"].append(ckv[:n_p].reshape(B, S, rkv))
        outs["kpe_p"].append(kpe[:n_p, :rope].reshape(B, S, rope))
        outs["C_p"].append(C_p[0]); outs["n_p"].append(nn_p); outs["m_p"].append(m_p)
        outs["ckv_s"].append(unpad(ckv))
        outs["kpe_s"].append(unpad(kpe)[:, :, :rope])
        outs["n_s"].append(nn_s); outs["m_s"].append(m_s)

    g_fin = g_final.reshape(1, D)
    y_prompt = _final_norm(xs, g_fin, row0=0, nrows=n_p, tm=tm, name="final_norm_prompt").reshape(B, S, D)
    y_sample = _final_norm(xs, g_fin, row0=n_p, nrows=n_s, tm=tm, name="final_norm_sample").reshape(
        Bd, SAMPLE_PAD, D)[:, :T]
    st = lambda k: jnp.stack(outs[k])
    return (y_prompt, y_sample, st("ckv_p"), st("kpe_p"), st("C_p"), st("n_p"), st("m_p"),
            st("ckv_s"), st("kpe_s"), c_sample, st("n_s"), st("m_s"))
```

```python
import functools
import math

import jax
import jax.numpy as jnp
import numpy as np
from jax import lax
from jax.experimental import pallas as pl
from jax.experimental.pallas import tpu as pltpu

EPS = 1e-6
ROPE_THETA = 10000.0
F32 = jnp.float32
BF16 = jnp.bfloat16

LANES = 128
SUBLANES = 8
MXU_COLS = 256
SAMPLE_PAD = SUBLANES
VMEM_LIMIT = 56 * 1024 * 1024
MLSTM_CHUNK = 256
MLSTM_SEQS_PER_STEP = 4
ATTN_TQ = 128
ATTN_TK = 512
ATTN_CHUNK = 2 * MXU_COLS
PAGES_PER_STEP = 32
INPROJ_TM = 1024
OUT_PLE_TM = 256

NT_DIMS = (((1,), (1,)), ((), ()))
TN_DIMS = (((0,), (0,)), ((), ()))


def _params(*sem):
    return pltpu.CompilerParams(dimension_semantics=sem, vmem_limit_bytes=VMEM_LIMIT)


def _pick_tile(n, prefs):
    for t in prefs:
        if n % t == 0:
            return t
    return n


def _sigmoid(x):
    return 1.0 / (1.0 + jnp.exp(-x))


def _log_sigmoid(x):
    return jnp.minimum(x, 0.0) - jnp.log(1.0 + jnp.exp(-jnp.abs(x)))


def _rms(x, g):
    return x * lax.rsqrt(jnp.mean(x * x, axis=-1, keepdims=True) + EPS) * g


_ANY = pl.BlockSpec(memory_space=pl.ANY)


def _norm_matmul_kernel(x_ref, g_ref, w_ref, ws_ref, o_ref, os_ref, h_ref):
    @pl.when(pl.program_id(1) == 0)
    def _():
        h_ref[...] = _rms(x_ref[...], g_ref[...]).astype(BF16)
        os_ref[...] = jnp.dot(h_ref[...], ws_ref[...], preferred_element_type=F32)

    o_ref[...] = jnp.dot(h_ref[...], w_ref[...], preferred_element_type=F32).astype(o_ref.dtype)


def _norm_matmul(x, g, w, w_small, *, tm, tn, name):
    m, k = x.shape
    n, ns = w.shape[1], w_small.shape[1]
    return pl.pallas_call(
        _norm_matmul_kernel,
        grid=(m // tm, n // tn),
        in_specs=[pl.BlockSpec((tm, k), lambda i, j: (i, 0)),
                  pl.BlockSpec((1, k), lambda i, j: (0, 0)),
                  pl.BlockSpec((k, tn), lambda i, j: (0, j)),
                  pl.BlockSpec((k, ns), lambda i, j: (0, 0))],
        out_specs=[pl.BlockSpec((tm, tn), lambda i, j: (i, j)),
                   pl.BlockSpec((tm, ns), lambda i, j: (i, 0))],
        out_shape=[jax.ShapeDtypeStruct((m, n), BF16), jax.ShapeDtypeStruct((m, ns), F32)],
        scratch_shapes=[pltpu.VMEM((tm, k), BF16)],
        compiler_params=_params("parallel", "arbitrary"),
        name=name,
    )(x, g, w, w_small)


def _mlstm_kernel(*refs, H, dk, dv, L, BB, t_valid, nc, has_init, n_alias, gate_lane):
    q_ref, k_ref, v_ref, o_ref, z_ref, s_ref, bias_ref, gh_ref = refs[:8]
    pos = 8
    if has_init:
        c0_ref, n0_ref, m0_ref = refs[pos:pos + 3]
        pos += 3
    pos += n_alias
    h_out, c_out, n_out, m_out, c_s, n_s, m_s = refs[pos:]
    c = pl.program_id(1)

    @pl.when(c == 0)
    def _():
        if has_init:
            c_s[...] = c0_ref[0]
            n_s[...] = n0_ref[0]
            m_s[...] = m0_ref[0]
        else:
            c_s[...] = jnp.zeros_like(c_s)
            n_s[...] = jnp.zeros_like(n_s)
            m_s[...] = jnp.zeros_like(m_s)

    row = lax.broadcasted_iota(jnp.int32, (L, L), 0)
    col = lax.broadcasted_iota(jnp.int32, (L, L), 1)
    causal = col <= row
    eye = col == row
    tok = lax.broadcasted_iota(jnp.int32, (L, 1), 0)
    head_row = lax.broadcasted_iota(jnp.int32, (H, dk), 0)
    head_lane = lax.broadcasted_iota(jnp.int32, (1, H), 1)
    scale = dk ** -0.5

    for bb, h in [(bb, h) for bb in range(BB) for h in range(H)]:
        r = slice(bb * L, (bb + 1) * L)
        if h == 0:
            n_all, m_all = n_s[bb], m_s[bb]
            n_acc, m_acc = n_all, m_all
            gates = s_ref[r, :] + bias_ref[...]
            csum = jnp.dot(causal.astype(F32), _log_sigmoid(gates), preferred_element_type=F32,
                           precision=lax.Precision.HIGHEST)
        qb = q_ref[r, h * dk:(h + 1) * dk]
        kb = k_ref[r, h * dk:(h + 1) * dk]
        vb = v_ref[r, h * dv:(h + 1) * dv]
        icol = gates[:, gate_lane + h:gate_lane + h + 1]
        bcol = csum[:, gate_lane + H + h:gate_lane + H + h + 1]
        wcol = icol - bcol
        wrow = jnp.sum(jnp.where(eye, wcol, 0.0), axis=0, keepdims=True)
        d = jnp.where(causal, bcol + wrow, -jnp.inf)
        m_prev = m_all[:, h:h + 1]
        inter = bcol + m_prev
        m_t = jnp.maximum(inter, jnp.max(d, axis=-1, keepdims=True))
        s_inter = jnp.exp(inter - m_t) * scale
        a = lax.dot_general(qb, kb, NT_DIMS, preferred_element_type=F32) * (jnp.exp(d - m_t) * scale)
        c_old = c_s[bb, h]
        num = (s_inter * lax.dot_general(qb, c_old.astype(BF16), NT_DIMS, preferred_element_type=F32)
               + jnp.dot(a.astype(BF16), vb, preferred_element_type=F32))
        n_old = n_all[h:h + 1, :]
        den = (s_inter * jnp.sum(qb.astype(F32) * n_old, axis=-1, keepdims=True)
               + jnp.sum(a, axis=-1, keepdims=True))
        hval = num / jnp.maximum(jnp.abs(den), jnp.exp(-m_t))

        m_new = m_t[t_valid - 1:t_valid, :]
        b_last = bcol[t_valid - 1:t_valid, :]
        g_state = jnp.exp(b_last + m_prev - m_new)
        g_tok = jnp.exp(wcol + (b_last - m_new))
        if t_valid < L:
            g_tok = jnp.where(tok < t_valid, g_tok, 0.0)
        vg = (vb.astype(F32) * g_tok).astype(BF16)
        c_s[bb, h] = g_state * c_old + lax.dot_general(vg, kb, TN_DIMS, preferred_element_type=F32)
        n_new = g_state * n_old + jnp.sum(kb.astype(F32) * g_tok, axis=0, keepdims=True)
        n_acc = jnp.where(head_row == h, n_new, n_acc)
        m_acc = jnp.where(head_lane == h, m_new, m_acc)
        if h == H - 1:
            n_s[bb] = n_acc
            m_s[bb] = m_acc

        hg = hval * _sigmoid(o_ref[r, h * dv:(h + 1) * dv].astype(F32))
        hn = _rms(hg, gh_ref[:, h * dv:(h + 1) * dv])
        z = z_ref[r, h * dv:(h + 1) * dv].astype(F32)
        h_out[r, h * dv:(h + 1) * dv] = (hn * (z * _sigmoid(z))).astype(BF16)

    @pl.when(c == nc - 1)
    def _():
        c_out[0] = c_s[...]
        n_out[...] = n_s[...]
        m_out[...] = m_s[...]


def _mlstm(u, us, gate_bias, g_head, init, h_buf, c_buf, *, layer, c_layers, rows_total, row0, B, S, L, t_valid,
           H, dk, dv, col, gate_lane, name):
    nc = S // L
    BB = _pick_tile(B, (MLSTM_SEQS_PER_STEP, 2, 1)) if nc == 1 else 1
    rb = BB * L
    assert row0 % rb == 0
    blk0 = row0 // rb
    hq, hv = H * dk, H * dv

    def rows(width, off):
        return pl.BlockSpec((rb, width), lambda b, c: (blk0 + b * nc + c, off // width))

    in_specs = [rows(hq, col["q"]), rows(hq, col["k"]), rows(hv, col["v"]), rows(hv, col["o"]),
                rows(hv, col["za"]), rows(LANES, 0),
                pl.BlockSpec((1, LANES), lambda b, c: (0, 0)),
                pl.BlockSpec((1, hv), lambda b, c: (0, 0))]
    args = [u, u, u, u, u, us, gate_bias, g_head]
    if init is not None:
        c0, n0, m0 = init
        in_specs += [pl.BlockSpec((1, BB, H, dv, dk), lambda b, c: (layer, b, 0, 0, 0)),
                     pl.BlockSpec((1, BB, H, dk), lambda b, c: (layer, b, 0, 0)),
                     pl.BlockSpec((1, BB, 1, H), lambda b, c: (layer, b, 0, 0))]
        args += [c0, n0, m0.reshape(m0.shape[0], B, 1, H)]
    aliases = {}
    for buf, out_idx in ((h_buf, 0), (c_buf, 1)):
        if buf is not None:
            aliases[len(args)] = out_idx
            in_specs.append(_ANY)
            args.append(buf)
    kern = functools.partial(_mlstm_kernel, H=H, dk=dk, dv=dv, L=L, BB=BB, t_valid=t_valid, nc=nc,
                             has_init=init is not None, n_alias=len(aliases), gate_lane=gate_lane)
    h_out, c_out, n_out, m_out = pl.pallas_call(
        kern,
        grid=(B // BB, nc),
        in_specs=in_specs,
        out_specs=[pl.BlockSpec((rb, hv), lambda b, c: (blk0 + b * nc + c, 0)),
                   pl.BlockSpec((1, BB, H, dv, dk), lambda b, c: (layer, b, 0, 0, 0)),
                   pl.BlockSpec((BB, H, dk), lambda b, c: (b, 0, 0)),
                   pl.BlockSpec((BB, 1, H), lambda b, c: (b, 0, 0))],
        out_shape=[jax.ShapeDtypeStruct((rows_total, hv), BF16),
                   jax.ShapeDtypeStruct((c_layers, B, H, dv, dk), F32),
                   jax.ShapeDtypeStruct((B, H, dk), F32),
                   jax.ShapeDtypeStruct((B, 1, H), F32)],
        scratch_shapes=[pltpu.VMEM((BB, H, dv, dk), F32), pltpu.VMEM((BB, H, dk), F32),
                        pltpu.VMEM((BB, 1, H), F32)],
        input_output_aliases=aliases,
        compiler_params=_params("parallel", "arbitrary"),
        name=name,
    )(*args)
    return h_out, c_out, n_out, m_out.reshape(B, H)


def _rope_lanes(x, cos_t, sin_lo, sin_hi):
    return x * cos_t + pltpu.roll(x, 96, 1) * sin_lo + pltpu.roll(x, 32, 1) * sin_hi


def _mla_prep_kernel(cq_ref, ckv_ref, s_ref, cos_ref, slo_ref, shi_ref, gq_ref, gkv_ref, wq_ref,
                     qn_out, qpe_out, ckv_out, ckvb_out, kpe_out, kpeb_out, *, n_nope, n_pe, rope, scale):
    cos_t, sin_lo, sin_hi = cos_ref[...], slo_ref[...], shi_ref[...]
    cqn = _rms(cq_ref[...].astype(F32), gq_ref[...]).astype(BF16)
    qf = jnp.dot(cqn, wq_ref[...], preferred_element_type=F32)
    qn_out[...] = qf[:, :n_nope].astype(BF16)
    for c in range(n_pe // LANES):
        blk = qf[:, n_nope + c * LANES:n_nope + (c + 1) * LANES]
        roped = (_rope_lanes(blk, cos_t, sin_lo, sin_hi) * scale).astype(BF16)
        for g in range(LANES // rope):
            qpe_out[c * (LANES // rope) + g] = roped[:, g * rope:(g + 1) * rope]
    ckv = _rms(ckv_ref[...].astype(F32), gkv_ref[...])
    ckv_out[...] = ckv
    ckvb_out[...] = ckv.astype(BF16)
    lane = lax.broadcasted_iota(jnp.int32, s_ref.shape, 1)
    kpe = jnp.where(lane < rope, _rope_lanes(s_ref[...], cos_t, sin_lo, sin_hi), 0.0)
    kpe_out[...] = kpe
    kpeb_out[...] = kpe.astype(BF16)


def _mla_prep(u, us, tables, g_q, g_kv, w_q, *, tm, rq, rkv, col, n_nope, n_pe, rope, scale, name):
    rows = u.shape[0]
    cos_t, sin_lo, sin_hi = tables
    tok = lambda width, off: pl.BlockSpec((tm, width), lambda i: (i, off // width))
    const = lambda a: pl.BlockSpec(a.shape, lambda i: (0,) * a.ndim)
    kern = functools.partial(_mla_prep_kernel, n_nope=n_nope, n_pe=n_pe, rope=rope, scale=scale)
    return pl.pallas_call(
        kern,
        grid=(rows // tm,),
        in_specs=[tok(rq, col["cq"]), tok(rkv, col["ckv"]), tok(LANES, 0), tok(LANES, 0), tok(LANES, 0),
                  tok(LANES, 0), const(g_q), const(g_kv), const(w_q)],
        out_specs=[tok(n_nope, 0), pl.BlockSpec((n_pe // rope, tm, rope), lambda i: (0, i, 0)),
                   tok(rkv, 0), tok(rkv, 0), tok(LANES, 0), tok(LANES, 0)],
        out_shape=[jax.ShapeDtypeStruct((rows, n_nope), BF16), jax.ShapeDtypeStruct((n_pe // rope, rows, rope), BF16),
                   jax.ShapeDtypeStruct((rows, rkv), F32), jax.ShapeDtypeStruct((rows, rkv), BF16),
                   jax.ShapeDtypeStruct((rows, LANES), F32), jax.ShapeDtypeStruct((rows, LANES), BF16)],
        compiler_params=_params("parallel"),
        name=name,
    )(u, u, us, cos_t, sin_lo, sin_hi, g_q, g_kv, w_q)


def _attn_prompt_kernel(qn_ref, qpe_ref, ckv_ref, ckvt_ref, kpe_ref, wuk_ref, wuv_ref, zb_ref, o_ref,
                        qlat_s, m_s, l_s, acc_s, *, H, tq, tk, nope, vdim, scale, chunk):
    qi = pl.program_id(1)
    cols = H * tq
    for h in range(H):
        ql = jnp.dot(qn_ref[:, h * nope:(h + 1) * nope], wuk_ref[h], preferred_element_type=F32)
        qlat_s[h * tq:(h + 1) * tq, :] = (ql * scale).astype(BF16)
    m_s[...] = jnp.full(m_s.shape, -jnp.inf, F32)
    l_s[...] = jnp.zeros_like(l_s)
    acc_s[...] = jnp.zeros_like(acc_s)

    def step(kb, width, masked):
        ks = pl.multiple_of(kb * tk, tk)
        kc = ckv_ref[pl.ds(ks, width), :]
        kp = kpe_ref[pl.ds(ks, width), :]
        kct = ckvt_ref[kb, :, :width]

        def scores(c0):
            qp = qpe_ref[c0 // tq:(c0 + chunk) // tq].reshape(chunk, qpe_ref.shape[-1])
            return (lax.dot_general(kc, qlat_s[c0:c0 + chunk, :], NT_DIMS, preferred_element_type=F32)
                    + lax.dot_general(kp, qp, NT_DIMS, preferred_element_type=F32))

        s_next = scores(0)
        for c0 in range(0, cols, chunk):
            sl = slice(c0, c0 + chunk)
            s = s_next
            if c0 + chunk < cols:
                s_next = scores(c0 + chunk)
            if masked:
                k_pos = ks + lax.broadcasted_iota(jnp.int32, (width, chunk), 0)
                q_pos = qi * tq + (lax.broadcasted_iota(jnp.int32, (width, chunk), 1) & (tq - 1))
                s = jnp.where(k_pos <= q_pos, s, -jnp.inf)
            m_old = m_s[:, sl]
            m_new = jnp.maximum(m_old, jnp.max(s, axis=0, keepdims=True))
            alpha = jnp.exp(m_old - m_new)
            p = jnp.exp(s - m_new)
            l_s[:, sl] = alpha * l_s[:, sl] + jnp.sum(p, axis=0, keepdims=True)
            acc_s[:, sl] = alpha * acc_s[:, sl] + jnp.dot(kct, p.astype(BF16), preferred_element_type=F32)
            m_s[:, sl] = m_new

    n_full = (qi * tq) // tk

    def body(kb, carry):
        step(kb, tk, False)
        return carry

    lax.fori_loop(0, n_full, body, 0)
    q_in_blk = qi % (tk // tq)
    for r in range(tk // tq):
        pl.when(q_in_blk == r)(functools.partial(step, n_full, (r + 1) * tq, True))

    for h in range(H):
        sl = slice(h * tq, (h + 1) * tq)
        o_t = (acc_s[:, sl] / l_s[:, sl]).astype(BF16)
        ob = lax.dot_general(o_t, wuv_ref[h], TN_DIMS, preferred_element_type=F32)
        z = zb_ref[:, h * vdim:(h + 1) * vdim].astype(F32)
        o_ref[:, h * vdim:(h + 1) * vdim] = (ob * (z * _sigmoid(z))).astype(BF16)


def _attn_prompt(qn, qpe_h, ckv_b, ckv_t, kpe_b, wuk, wuv, u, *, rows_total, B, S, H, tq, tk, col, scale, name):
    nq = S // tq
    nope, rkv = wuk.shape[1], wuk.shape[2]
    vdim = wuv.shape[2]
    rp = qpe_h.shape[-1]
    hv = H * vdim
    chunk = _pick_tile(H * tq, (ATTN_CHUNK, 2 * tq, tq))
    kern = functools.partial(_attn_prompt_kernel, H=H, tq=tq, tk=tk, nope=nope, vdim=vdim, scale=scale, chunk=chunk)
    once = pl.Buffered(1)
    return pl.pallas_call(
        kern,
        grid=(B, nq),
        in_specs=[pl.BlockSpec((tq, H * nope), lambda b, q: (b * nq + q, 0)),
                  pl.BlockSpec((H, tq, rp), lambda b, q: (0, b * nq + q, 0)),
                  pl.BlockSpec((S, rkv), lambda b, q: (b, 0), pipeline_mode=once),
                  pl.BlockSpec((S // tk, rkv, tk), lambda b, q: (b, 0, 0), pipeline_mode=once),
                  pl.BlockSpec((S, rp), lambda b, q: (b, 0), pipeline_mode=once),
                  pl.BlockSpec(wuk.shape, lambda b, q: (0, 0, 0), pipeline_mode=once),
                  pl.BlockSpec(wuv.shape, lambda b, q: (0, 0, 0), pipeline_mode=once),
                  pl.BlockSpec((tq, hv), lambda b, q: (b * nq + q, col["zb"] // hv))],
        out_specs=pl.BlockSpec((tq, hv), lambda b, q: (b * nq + q, 0)),
        out_shape=jax.ShapeDtypeStruct((rows_total, hv), BF16),
        scratch_shapes=[pltpu.VMEM((H * tq, rkv), BF16), pltpu.VMEM((1, H * tq), F32),
                        pltpu.VMEM((1, H * tq), F32), pltpu.VMEM((rkv, H * tq), F32)],
        compiler_params=_params("parallel", "arbitrary"),
        name=name,
    )(qn, qpe_h, ckv_b, ckv_t, kpe_b, wuk, wuv, u)


def _qlat_kernel(qn_ref, wuk_ref, o_ref, *, scale):
    o_ref[0] = (jnp.dot(qn_ref[...], wuk_ref[0], preferred_element_type=F32) * scale).astype(BF16)


def _qlat(qn, wuk, *, row0, rows, scale, name):
    H, nope, rkv = wuk.shape
    assert row0 % rows == 0
    return pl.pallas_call(
        functools.partial(_qlat_kernel, scale=scale),
        grid=(H,),
        in_specs=[pl.BlockSpec((rows, nope), lambda h: (row0 // rows, h)),
                  pl.BlockSpec((1, nope, rkv), lambda h: (h, 0, 0))],
        out_specs=pl.BlockSpec((1, rows, rkv), lambda h: (h, 0, 0)),
        out_shape=jax.ShapeDtypeStruct((H, rows, rkv), BF16),
        compiler_params=_params("parallel"),
        name=name,
    )(qn, wuk)


def _ouv_kernel(ol_ref, wuv_ref, zb_ref, buf_ref, o_ref):
    del buf_ref
    ob = jnp.dot(ol_ref[0], wuv_ref[0], preferred_element_type=F32)
    z = zb_ref[...].astype(F32)
    o_ref[...] = (ob * (z * _sigmoid(z))).astype(BF16)


def _ouv(o_lat, wuv, u, ob_buf, *, row0, rows, zb_col, name):
    H, rkv, vdim = wuv.shape
    return pl.pallas_call(
        _ouv_kernel,
        grid=(H,),
        in_specs=[pl.BlockSpec((1, rows, rkv), lambda h: (h, 0, 0)),
                  pl.BlockSpec((1, rkv, vdim), lambda h: (h, 0, 0)),
                  pl.BlockSpec((rows, vdim), lambda h: (row0 // rows, zb_col // vdim + h)),
                  _ANY],
        out_specs=pl.BlockSpec((rows, vdim), lambda h: (row0 // rows, h)),
        out_shape=jax.ShapeDtypeStruct(ob_buf.shape, BF16),
        input_output_aliases={3: 0},
        compiler_params=_params("parallel"),
        name=name,
    )(o_lat, wuv, u, ob_buf)


def _attn_sample_kernel(pt_ref, ql_ref, qp_ref, cn_ref, kn_ref, ckv_hbm, kpe_hbm, o_ref,
                        ckv_buf, kpe_buf, sems, m_s, l_s, acc_s, *, layer, P, T, n_batch, n_steps, n_pages):
    page = ckv_buf.shape[1] // P
    b = pl.program_id(0)
    j = pl.program_id(1)
    step = b * n_steps + j
    slot = step % 2

    def page_copies(bb, jj, sl):
        out = []
        for i in range(P):
            pid = pt_ref[bb * n_pages + jj * P + i]
            out.append(pltpu.make_async_copy(ckv_hbm.at[layer, pid], ckv_buf.at[sl, pl.ds(i * page, page)],
                                             sems.at[sl, 0]))
            out.append(pltpu.make_async_copy(kpe_hbm.at[layer, pid], kpe_buf.at[sl, i], sems.at[sl, 1]))
        return out

    @pl.when(step == 0)
    def _():
        for cp in page_copies(b, j, slot):
            cp.start()

    @pl.when(step + 1 < n_batch * n_steps)
    def _():
        wrap = j == n_steps - 1
        for cp in page_copies(jnp.where(wrap, b + 1, b), jnp.where(wrap, 0, j + 1), 1 - slot):
            cp.start()

    for cp in page_copies(b, j, slot):
        cp.wait()

    ql = ql_ref[0]
    qp = qp_ref[0]

    @pl.when(j == 0)
    def _():
        m_s[...] = jnp.full(m_s.shape, -jnp.inf, F32)
        l_s[...] = jnp.zeros_like(l_s)
        acc_s[...] = jnp.zeros_like(acc_s)

    def update(s, v):
        m_old = m_s[...]
        m_new = jnp.maximum(m_old, jnp.max(s, axis=-1, keepdims=True))
        alpha = jnp.exp(m_old - m_new)
        p = jnp.exp(s - m_new)
        l_s[...] = alpha * l_s[...] + jnp.sum(p, axis=-1, keepdims=True)
        acc_s[...] = alpha * acc_s[...] + jnp.dot(p.astype(BF16), v, preferred_element_type=F32)
        m_s[...] = m_new

    kc = ckv_buf[slot].astype(BF16)
    kpt = jnp.concatenate([kpe_buf[slot, i].astype(BF16) for i in range(P)], axis=1)
    update(lax.dot_general(ql, kc, NT_DIMS, preferred_element_type=F32)
           + jnp.dot(qp, kpt, preferred_element_type=F32), kc)

    @pl.when(j == n_steps - 1)
    def _():
        cn = cn_ref[0]
        kn = kn_ref[0]
        s = (lax.dot_general(ql, cn, NT_DIMS, preferred_element_type=F32)
             + lax.dot_general(qp, kn, NT_DIMS, preferred_element_type=F32))
        t_q = lax.broadcasted_iota(jnp.int32, s.shape, 0) & (T - 1)
        t_k = lax.broadcasted_iota(jnp.int32, s.shape, 1)
        update(jnp.where(t_k <= t_q, s, -jnp.inf), cn)
        o_ref[0] = (acc_s[...] / l_s[...]).astype(BF16)


def _attn_sample(page_table, ql, qp, cn, kn, cache_ckv, cache_kpe_t, *, layer, T, P, name):
    Bd, rows, rkv = ql.shape
    rope = qp.shape[-1]
    n_pages = page_table.shape[1]
    page = cache_ckv.shape[2]
    n_steps = n_pages // P
    assert T & (T - 1) == 0 and n_pages % P == 0

    per_b = lambda shape: pl.BlockSpec((1,) + shape, lambda b, j, pt: (b, 0, 0))
    grid_spec = pltpu.PrefetchScalarGridSpec(
        num_scalar_prefetch=1,
        grid=(Bd, n_steps),
        in_specs=[per_b((rows, rkv)), per_b((rows, rope)), per_b((SAMPLE_PAD, rkv)), per_b((SAMPLE_PAD, rope)),
                  _ANY, _ANY],
        out_specs=per_b((rows, rkv)),
        scratch_shapes=[pltpu.VMEM((2, P * page, rkv), F32),
                        pltpu.VMEM((2, P, rope, page), F32),
                        pltpu.SemaphoreType.DMA((2, 2)),
                        pltpu.VMEM((rows, 1), F32), pltpu.VMEM((rows, 1), F32), pltpu.VMEM((rows, rkv), F32)],
    )
    kern = functools.partial(_attn_sample_kernel, layer=layer, P=P, T=T, n_batch=Bd, n_steps=n_steps,
                             n_pages=n_pages)
    return pl.pallas_call(
        kern,
        grid_spec=grid_spec,
        out_shape=jax.ShapeDtypeStruct((Bd, rows, rkv), BF16),
        compiler_params=_params("arbitrary", "arbitrary"),
        name=name,
    )(page_table.reshape(-1), ql, qp, cn, kn, cache_ckv, cache_kpe_t)


def _mix_kernel(ha_ref, ob_ref, wpa_ref, wpb_ref, ga_ref, gb_ref, o_ref):
    ya = jnp.dot(ha_ref[...], wpa_ref[...], preferred_element_type=F32)
    yb = jnp.dot(ob_ref[...], wpb_ref[...], preferred_element_type=F32)
    o_ref[...] = (_sigmoid(ga_ref[...].astype(F32)) * ya + _sigmoid(gb_ref[...].astype(F32)) * yb).astype(BF16)


def _mix(ha, ob, wpa, wpb, u, *, tm, tn, col, name):
    rows, ka = ha.shape
    kb = ob.shape[1]
    n = wpa.shape[1]
    return pl.pallas_call(
        _mix_kernel,
        grid=(rows // tm, n // tn),
        in_specs=[pl.BlockSpec((tm, ka), lambda i, j: (i, 0)),
                  pl.BlockSpec((tm, kb), lambda i, j: (i, 0)),
                  pl.BlockSpec((ka, tn), lambda i, j: (0, j)),
                  pl.BlockSpec((kb, tn), lambda i, j: (0, j)),
                  pl.BlockSpec((tm, tn), lambda i, j: (i, col["ga"] // tn + j)),
                  pl.BlockSpec((tm, tn), lambda i, j: (i, col["gb"] // tn + j))],
        out_specs=pl.BlockSpec((tm, tn), lambda i, j: (i, j)),
        out_shape=jax.ShapeDtypeStruct((rows, n), BF16),
        compiler_params=_params("parallel", "arbitrary"),
        name=name,
    )(ha, ob, wpa, wpb, u, u)


def _out_ple_kernel(mix_ref, wout_ref, x_ref, g_ref, wpg_ref, p_ref, wple_ref, o_ref):
    x1 = x_ref[...] + jnp.dot(mix_ref[...], wout_ref[...], preferred_element_type=F32)
    gate = _sigmoid(jnp.dot(_rms(x1, g_ref[...]).astype(BF16), wpg_ref[...], preferred_element_type=F32))
    emb = jnp.dot(p_ref[...].astype(BF16), wple_ref[...], preferred_element_type=F32)
    o_ref[...] = x1 + emb * gate


def _out_ple(mix, wout, x, g, wpg, p, wple, *, tm, name):
    rows, d = x.shape
    pd = p.shape[1]
    once = pl.Buffered(1)
    row_tile = lambda width: pl.BlockSpec((tm, width), lambda i: (i, 0))
    const = lambda a: pl.BlockSpec(a.shape, lambda i: (0, 0), pipeline_mode=once)
    return pl.pallas_call(
        _out_ple_kernel,
        grid=(rows // tm,),
        in_specs=[row_tile(d), const(wout), row_tile(d), const(g), const(wpg), row_tile(pd), const(wple)],
        out_specs=row_tile(d),
        out_shape=jax.ShapeDtypeStruct((rows, d), F32),
        compiler_params=_params("parallel"),
        name=name,
    )(mix, wout, x, g, wpg, p, wple)


def _final_norm_kernel(x_ref, g_ref, o_ref):
    o_ref[...] = _rms(x_ref[...], g_ref[...])


def _final_norm(x, g, *, row0, nrows, tm, name):
    d = x.shape[1]
    blk0 = row0 // tm
    return pl.pallas_call(
        _final_norm_kernel,
        grid=(nrows // tm,),
        in_specs=[pl.BlockSpec((tm, d), lambda i: (blk0 + i, 0)), pl.BlockSpec((1, d), lambda i: (0, 0))],
        out_specs=pl.BlockSpec((tm, d), lambda i: (i, 0)),
        out_shape=jax.ShapeDtypeStruct((nrows, d), F32),
        compiler_params=_params("parallel"),
        name=name,
    )(x, g)


def _rope_tables(pos, rope):
    half = rope // 2
    inv = ROPE_THETA ** (-jnp.arange(half, dtype=F32) / half)
    ang = pos[:, None] * inv[None, :]
    cos, sin, zero = jnp.cos(ang), jnp.sin(ang), jnp.zeros_like(ang)
    reps = LANES // rope
    cos_t = jnp.tile(jnp.concatenate([cos, cos], axis=1), (1, reps))
    sin_lo = jnp.tile(jnp.concatenate([-sin, zero], axis=1), (1, reps))
    sin_hi = jnp.tile(jnp.concatenate([zero, sin], axis=1), (1, reps))
    return cos_t, sin_lo, sin_hi


def kernel(x_prompt, x_sample, p_prompt, p_sample, cache_ckv, cache_kpe, state_C, state_n, state_m,
           page_table, g_in, w_in, b_i, b_f, g_q, w_uq, g_kv, w_uk, w_uv, g_head, w_pa, w_pb,
           w_out, w_ple, g_ple, w_pg, g_final):
    B, S, D = x_prompt.shape
    Bd, T, _ = x_sample.shape
    depth = g_in.shape[0]
    H = b_i.shape[1]
    dv = D // H
    dk = dv // 2
    hq, hv = H * dk, H * dv
    rq, rkv = g_q.shape[1], g_kv.shape[1]
    HB = w_uk.shape[2]
    nope, vdim = w_uk.shape[3], w_uv.shape[3]
    rope = w_uq.shape[3] - nope
    wb = HB * vdim
    page = cache_ckv.shape[2]
    past_len = page_table.shape[1] * page
    mla_scale = float((nope + rope) ** -0.5)
    assert T <= SAMPLE_PAD and LANES % rope == 0 and rope + 2 * H <= LANES
    assert hv == D and wb == D and rq == rkv and (7 * D) % rq == 0

    n_p, n_s = B * S, Bd * SAMPLE_PAD
    rows = n_p + n_s
    tm = _pick_tile(math.gcd(n_p, n_s), (512, 256, 128, 64, 32, 16, 8))
    tm_in = _pick_tile(math.gcd(n_p, n_s), (INPROJ_TM, 512, 256, 128, 64, 32, 16, 8))

    pad_t = ((0, 0), (0, SAMPLE_PAD - T), (0, 0))
    x = jnp.concatenate([x_prompt.reshape(n_p, D), jnp.pad(x_sample, pad_t).reshape(n_s, D)], axis=0)
    pos = jnp.concatenate([jnp.tile(jnp.arange(S, dtype=F32), B),
                           jnp.tile(past_len + jnp.arange(SAMPLE_PAD, dtype=F32), Bd)])
    tables = _rope_tables(pos, rope)
    cache_kpe_t = jnp.swapaxes(cache_kpe, 2, 3)

    col = {"v": 0, "o": D, "za": 2 * D, "zb": 3 * D, "ga": 4 * D, "gb": 5 * D,
           "q": 6 * D, "k": 6 * D + hq, "cq": 7 * D, "ckv": 7 * D + rq}
    n_main = 7 * D + rq + rkv
    sizes = (hq, hq, hv, H, H, hv, hv, rq, rkv, rope, wb, D, D)
    starts = np.concatenate([[0], np.cumsum(sizes)])
    seg = {name: (int(starts[i]), int(starts[i + 1])) for i, name in enumerate(
        ("q", "k", "v", "i", "f", "o", "za", "cq", "ckv", "kr", "zb", "ga", "gb"))}
    main_order = ("v", "o", "za", "zb", "ga", "gb", "q", "k", "cq", "ckv")
    gate_lane = rope
    tn_main = _pick_tile(n_main, (1024, 512, 256, 128))
    tn_d = _pick_tile(D, (1024, 512, 256, 128))

    xs = x
    c_sample = None
    outs = {k: [] for k in ("ckv_p", "kpe_p", "C_p", "n_p", "m_p", "ckv_s", "kpe_s", "n_s", "m_s")}
    for l in range(depth):
        wl = w_in[l]
        w_main = jnp.concatenate([wl[:, seg[n][0]:seg[n][1]].astype(BF16) for n in main_order], axis=1)
        w_small = jnp.concatenate(
            [wl[:, seg[n][0]:seg[n][1]].astype(BF16) for n in ("kr", "i", "f")]
            + [jnp.zeros((D, LANES - rope - 2 * H), BF16)], axis=1)
        gate_bias = jnp.concatenate([jnp.zeros((rope,), F32), b_i[l], b_f[l],
                                     jnp.zeros((LANES - rope - 2 * H,), F32)]).reshape(1, LANES)
        wq = w_uq[l]
        w_q = jnp.concatenate([wq[:, :, :nope].reshape(rq, HB * nope),
                               wq[:, :, nope:].reshape(rq, HB * rope)], axis=1).astype(BF16)
        wuk = jnp.transpose(w_uk[l], (1, 2, 0)).astype(BF16)
        wuv = jnp.transpose(w_uv[l], (1, 0, 2)).astype(BF16)

        g_in_l = g_in[l].reshape(1, D)
        u, us = _norm_matmul(xs, g_in_l, w_main, w_small, tm=tm_in, tn=tn_main, name=f"inproj_{l}")

        g_head_l = g_head[l].reshape(1, hv)
        Lp = _pick_tile(S, (MLSTM_CHUNK, 128, 64, 32, 16, 8))
        common = dict(rows_total=rows, H=H, dk=dk, dv=dv, col=col, gate_lane=gate_lane)
        ha, C_p, nn_p, m_p = _mlstm(u, us, gate_bias, g_head_l, None, None, None, layer=0, c_layers=1,
                                    row0=0, B=B, S=S, L=Lp, t_valid=Lp, name=f"mlstm_prompt_{l}", **common)
        ha, c_sample, nn_s, m_s = _mlstm(u, us, gate_bias, g_head_l, (state_C, state_n, state_m), ha, c_sample,
                                         layer=l, c_layers=depth, row0=n_p, B=Bd, S=SAMPLE_PAD, L=SAMPLE_PAD,
                                         t_valid=T, name=f"mlstm_sample_{l}", **common)

        qn, qpe_h, ckv, ckv_b, kpe, kpe_b = _mla_prep(
            u, us, tables, g_q[l].reshape(1, rq), g_kv[l].reshape(1, rkv), w_q, tm=tm, rq=rq, rkv=rkv,
            col=col, n_nope=HB * nope, n_pe=HB * rope, rope=rope, scale=mla_scale, name=f"mla_prep_{l}")
        kpe_b = kpe_b[:, :rope]
        tq = _pick_tile(S, (ATTN_TQ, 64, 32, 16, 8))
        tk = _pick_tile(S, (ATTN_TK, 256, 128))
        ckv_t = jnp.transpose(ckv_b[:n_p].reshape(n_p // tk, tk, rkv), (0, 2, 1))
        ob = _attn_prompt(qn, qpe_h, ckv_b, ckv_t, kpe_b, wuk, wuv, u, rows_total=rows, B=B, S=S, H=HB,
                          tq=tq, tk=tk, col=col, scale=mla_scale, name=f"attn_prompt_{l}")

        ql_s = _qlat(qn, wuk, row0=n_p, rows=n_s, scale=mla_scale, name=f"qlat_sample_{l}")
        to_bt = lambda a: jnp.transpose(a.reshape(HB, Bd, SAMPLE_PAD, -1)[:, :, :T], (1, 0, 2, 3)).reshape(
            Bd, HB * T, a.shape[-1])
        ol_s = _attn_sample(page_table, to_bt(ql_s), to_bt(qpe_h[:, n_p:]),
                            ckv_b[n_p:].reshape(Bd, SAMPLE_PAD, rkv), kpe_b[n_p:].reshape(Bd, SAMPLE_PAD, rope),
                            cache_ckv, cache_kpe_t, layer=l, T=T,
                            P=_pick_tile(page_table.shape[1], (PAGES_PER_STEP, 16, 8, 4, 2, 1)),
                            name=f"attn_sample_{l}")
        ol_h = jnp.pad(jnp.transpose(ol_s.reshape(Bd, HB, T, rkv), (1, 0, 2, 3)),
                       ((0, 0), (0, 0), (0, SAMPLE_PAD - T), (0, 0))).reshape(HB, n_s, rkv)
        ob = _ouv(ol_h, wuv, u, ob, row0=n_p, rows=n_s, zb_col=col["zb"], name=f"ouv_sample_{l}")

        mix = _mix(ha, ob, w_pa[l].astype(BF16), w_pb[l].astype(BF16), u, tm=tm_in, tn=tn_d, col=col, name=f"mix_{l}")
        p_all = jnp.concatenate([p_prompt[l].reshape(n_p, -1),
                                 jnp.pad(p_sample[l], pad_t).reshape(n_s, -1)], axis=0)
        xs = _out_ple(mix, w_out[l].astype(BF16), xs, g_ple[l].reshape(1, D), w_pg[l].astype(BF16), p_all,
                      w_ple[l].astype(BF16), tm=_pick_tile(tm, (OUT_PLE_TM,)), name=f"out_ple_{l}")

        unpad = lambda a: a[n_p:].reshape(Bd, SAMPLE_PAD, -1)[:, :T]
        outs["ckv_p"].append(ckv[:n_p].reshape(B, S, rkv))
        outs["kpe_p"].append(kpe[:n_p, :rope].reshape(B, S, rope))
        outs["C_p"].append(C_p[0]); outs["n_p"].append(nn_p); outs["m_p"].append(m_p)
        outs["ckv_s"].append(unpad(ckv))
        outs["kpe_s"].append(unpad(kpe)[:, :, :rope])
        outs["n_s"].append(nn_s); outs["m_s"].append(m_s)

    g_fin = g_final.reshape(1, D)
    y_prompt = _final_norm(xs, g_fin, row0=0, nrows=n_p, tm=tm, name="final_norm_prompt").reshape(B, S, D)
    y_sample = _final_norm(xs, g_fin, row0=n_p, nrows=n_s, tm=tm, name="final_norm_sample").reshape(
        Bd, SAMPLE_PAD, D)[:, :T]
    st = lambda k: jnp.stack(outs[k])
    return (y_prompt, y_sample, st("ckv_p"), st("kpe_p"), st("C_p"), st("n_p"), st("m_p"),
            st("ckv_s"), st("kpe_s"), c_sample, st("n_s"), st("m_s"))
```
